```python
import math
import jax
import jax.numpy as jnp
from jax import lax
import numpy as np

D_MODEL = 1024
BATCH = 16
SEQ = 2048
DEPTH = 2

GRID_W = 64
CTX_LEN = 256
Q_BLOCK = 128
ROPE_THETA = 10000.0
RMS_EPS = 1e-6
LN_EPS = 1e-5

CONV_CH = D_MODEL // 2
CONV_WIDTH = 31
CONV_PAD = CONV_WIDTH // 2
DIFF_HEAD_DIM = 64
DIFF_V_DIM = 2 * DIFF_HEAD_DIM
DIFF_HEADS = (D_MODEL // 2) // DIFF_V_DIM
DIFF_QK = DIFF_HEADS * 2 * DIFF_HEAD_DIM
DIFF_SCALE = DIFF_HEAD_DIM ** -0.5
AB_IN = 2 * CONV_CH + 2 * DIFF_QK + DIFF_HEADS * DIFF_V_DIM
AB_OUT = CONV_CH + DIFF_HEADS * DIFF_V_DIM
GQA_HEAD_DIM = 128
GQA_HEADS = D_MODEL // GQA_HEAD_DIM
GQA_KV_HEADS = GQA_HEADS // 4
GQA_REP = GQA_HEADS // GQA_KV_HEADS
GQA_SCALE = GQA_HEAD_DIM ** -0.5
GQA_QKV = (GQA_HEADS + 2 * GQA_KV_HEADS) * GQA_HEAD_DIM
GQA_OUT = GQA_HEADS * GQA_HEAD_DIM
N_EXPERTS = 32
TOP_K = 4
EXPERT_FF = D_MODEL
SWIGLU_LIMIT = 7.0
SWIGLU_ALPHA = 1.702

kernel_name = "hybrid_conv_diffattn_gqa_moe_flow_block"


def rms_norm(x, g):
    xf = x.astype(jnp.float32)
    y = xf * lax.rsqrt(jnp.mean(xf * xf, axis=-1, keepdims=True) + RMS_EPS)
    return (y * g.astype(jnp.float32)).astype(x.dtype)


def layer_norm(x, g, b):
    xf = x.astype(jnp.float32)
    mu = jnp.mean(xf, axis=-1, keepdims=True)
    var = jnp.mean(jnp.square(xf - mu), axis=-1, keepdims=True)
    y = (xf - mu) * lax.rsqrt(var + LN_EPS)
    return (y * g.astype(jnp.float32) + b.astype(jnp.float32)).astype(x.dtype)


def modulate(x, g, shift, scale):
    return rms_norm(x, g) * (1.0 + scale) + shift


def axial_rope(n, head_dim, dtype):
    rows = n // GRID_W
    row = jnp.repeat(jnp.arange(rows, dtype=jnp.float32), GRID_W)
    col = jnp.tile(jnp.arange(GRID_W, dtype=jnp.float32), rows)
    quarter = head_dim // 4
    inv = ROPE_THETA ** (-jnp.arange(quarter, dtype=jnp.float32) / quarter)
    ang = jnp.stack([row[:, None] * inv, col[:, None] * inv], axis=1)
    return jnp.cos(ang).astype(dtype), jnp.sin(ang).astype(dtype)


def apply_axial_rope(x, cos, sin):
    hd = x.shape[-1]
    quarter = hd // 4
    extra = [1] * (x.ndim - 3)
    c = cos.reshape(cos.shape[0], *extra, 2, quarter)
    s = sin.reshape(sin.shape[0], *extra, 2, quarter)
    xs = x.reshape(*x.shape[:-1], 2, 2, quarter)
    x1 = xs[..., 0, :]
    x2 = xs[..., 1, :]
    out = jnp.stack([x1 * c - x2 * s, x2 * c + x1 * s], axis=-2)
    return out.reshape(x.shape)


def sweep_query_blocks(attend, q):
    b, n = q.shape[0], q.shape[1]
    nb = n // Q_BLOCK
    qb = jnp.moveaxis(q.reshape(b, nb, Q_BLOCK, *q.shape[2:]), 1, 0)
    out = lax.map(attend, qb)
    return jnp.moveaxis(out, 0, 1).reshape(b, n, *out.shape[3:])


def conv_branch(u, conv_w, conv_b, ln_g, ln_b):
    a, gate = jnp.split(u, 2, axis=-1)
    y = a * jax.nn.sigmoid(gate)
    y = lax.conv_general_dilated(
        y, conv_w[:, None, :], window_strides=(1,), padding=[(CONV_PAD, CONV_PAD)],
        dimension_numbers=("NWC", "WIO", "NWC"), feature_group_count=y.shape[-1]) + conv_b
    return jax.nn.silu(layer_norm(y, ln_g, ln_b))


def diff_attend_fn(k, v, lam):
    def attend(q):
        s = jnp.einsum("bqhmd,bkhmd->bhmqk", q, k, preferred_element_type=jnp.float32) * DIFF_SCALE
        p = jax.nn.softmax(s, axis=-1)
        w = p[:, :, 0] - lam * p[:, :, 1]
        return jnp.einsum("bhqk,bkhe->bqhe", w.astype(v.dtype), v)
    return attend


def gqa_attend_fn(k, v):
    def attend(q):
        s = jnp.einsum("bqgrd,bkgd->bgrqk", q, k, preferred_element_type=jnp.float32) * GQA_SCALE
        p = jax.nn.softmax(s, axis=-1)
        return jnp.einsum("bgrqk,bkgd->bqgrd", p.astype(v.dtype), v)
    return attend


def even_mixer(h_ctx, h_lat, w_in, w_out, conv_w, conv_b, ln_g, ln_b, lam_p, subln_g, lam_init,
               cos, sin, with_ctx):
    def project(h):
        u = h @ w_in
        b, l = u.shape[:2]
        glu_in = u[..., :2 * CONV_CH]
        o = 2 * CONV_CH
        q = u[..., o:o + DIFF_QK].reshape(b, l, DIFF_HEADS, 2, DIFF_HEAD_DIM)
        k = u[..., o + DIFF_QK:o + 2 * DIFF_QK].reshape(b, l, DIFF_HEADS, 2, DIFF_HEAD_DIM)
        v = u[..., o + 2 * DIFF_QK:].reshape(b, l, DIFF_HEADS, DIFF_V_DIM)
        return glu_in, q, k, v

    lp = lam_p.astype(jnp.float32)
    lam = jnp.exp(jnp.sum(lp[0] * lp[1])) - jnp.exp(jnp.sum(lp[2] * lp[3])) + lam_init

    glu_c, q_c, k_c, v_c = project(h_ctx)
    glu_l, q_l, k_l, v_l = project(h_lat)
    q_l = apply_axial_rope(q_l, cos, sin)
    k_l = apply_axial_rope(k_l, cos, sin)
    k_all = jnp.concatenate([k_c, k_l], axis=1)
    v_all = jnp.concatenate([v_c, v_l], axis=1)

    def finish(glu_in, attn):
        a = conv_branch(glu_in, conv_w, conv_b, ln_g, ln_b)
        d = rms_norm(attn, subln_g) * (1.0 - lam_init)
        d = d.reshape(*d.shape[:2], DIFF_HEADS * DIFF_V_DIM)
        return jnp.concatenate([a, d], axis=-1) @ w_out

    o_lat = finish(glu_l, sweep_query_blocks(diff_attend_fn(k_all, v_all, lam), q_l))
    o_ctx = finish(glu_c, diff_attend_fn(k_c, v_c, lam)(q_c)) if with_ctx else None
    return o_ctx, o_lat


def odd_mixer(h_ctx, h_lat, w_qkv, w_out, qn_g, kn_g, cos, sin, with_ctx):
    def project(h):
        u = h @ w_qkv
        b, l = u.shape[:2]
        nq = GQA_HEADS * GQA_HEAD_DIM
        nk = GQA_KV_HEADS * GQA_HEAD_DIM
        q = u[..., :nq].reshape(b, l, GQA_KV_HEADS, GQA_REP, GQA_HEAD_DIM)
        k = u[..., nq:nq + nk].reshape(b, l, GQA_KV_HEADS, GQA_HEAD_DIM)
        v = u[..., nq + nk:].reshape(b, l, GQA_KV_HEADS, GQA_HEAD_DIM)
        return rms_norm(q, qn_g), rms_norm(k, kn_g), v

    q_c, k_c, v_c = project(h_ctx)
    q_l, k_l, v_l = project(h_lat)
    q_l = apply_axial_rope(q_l, cos, sin)
    k_l = apply_axial_rope(k_l, cos, sin)
    k_all = jnp.concatenate([k_c, k_l], axis=1)
    v_all = jnp.concatenate([v_c, v_l], axis=1)

    def out_proj(o):
        return o.reshape(*o.shape[:2], GQA_OUT) @ w_out

    o_lat = out_proj(sweep_query_blocks(gqa_attend_fn(k_all, v_all), q_l))
    o_ctx = out_proj(gqa_attend_fn(k_c, v_c)(q_c)) if with_ctx else None
    return o_ctx, o_lat


def moe(h, router_w, router_b, w_in, b_in, w_out, b_out):
    logits = (h @ router_w + router_b).astype(jnp.float32)
    top_vals, top_idx = lax.top_k(logits, TOP_K)
    top_w = jax.nn.softmax(top_vals, axis=-1)
    gates = jnp.einsum("tk,tke->te", top_w,
                       jax.nn.one_hot(top_idx, N_EXPERTS, dtype=jnp.float32)).astype(h.dtype)
    out = jnp.zeros(h.shape, h.dtype)
    for e in range(N_EXPERTS):
        g, lin = jnp.split(h @ w_in[e] + b_in[e], 2, axis=-1)
        g = jnp.minimum(g, SWIGLU_LIMIT)
        lin = jnp.clip(lin, -SWIGLU_LIMIT, SWIGLU_LIMIT)
        act = g * jax.nn.sigmoid(SWIGLU_ALPHA * g) * (lin + 1.0)
        out = out + gates[:, e:e + 1] * (act @ w_out[e] + b_out[e])
    return out


def setup_inputs(seed: int = 0) -> dict:
    key = jax.random.key(seed)
    ks = iter(jax.random.split(key, 40))
    n_even = (DEPTH + 1) // 2
    n_odd = DEPTH // 2
    f32 = jnp.float32

    def nrm(shape, scale):
        return scale * jax.random.normal(next(ks), shape, f32)

    def gain(shape):
        return 1.0 + 0.02 * jax.random.normal(next(ks), shape, f32)

    d = D_MODEL
    return {
        "x": nrm((BATCH, SEQ, d), 1.0),
        "c": nrm((BATCH, d), 1.0),
        "ctx": nrm((BATCH, CTX_LEN, d), 1.0),
        "c_ctx": nrm((d,), 1.0),
        "mod_w": nrm((DEPTH, d, 6 * d), d ** -0.5),
        "mod_b": nrm((DEPTH, 6 * d), 0.02),
        "norm1_g": gain((DEPTH, d)),
        "norm2_g": gain((DEPTH, d)),
        "router_w": nrm((DEPTH, d, N_EXPERTS), d ** -0.5),
        "router_b": nrm((DEPTH, N_EXPERTS), 0.01),
        "expert_w_in": nrm((DEPTH, N_EXPERTS, d, 2 * EXPERT_FF), d ** -0.5),
        "expert_b_in": nrm((DEPTH, N_EXPERTS, 2 * EXPERT_FF), 0.02),
        "expert_w_out": nrm((DEPTH, N_EXPERTS, EXPERT_FF, d), EXPERT_FF ** -0.5),
        "expert_b_out": nrm((DEPTH, N_EXPERTS, d), 0.02),
        "ab_w_in": nrm((n_even, d, AB_IN), d ** -0.5),
        "ab_w_out": nrm((n_even, AB_OUT, d), AB_OUT ** -0.5),
        "conv_w": nrm((n_even, CONV_WIDTH, CONV_CH), CONV_WIDTH ** -0.5),
        "conv_b": nrm((n_even, CONV_CH), 0.02),
        "conv_ln_g": gain((n_even, CONV_CH)),
        "conv_ln_b": nrm((n_even, CONV_CH), 0.02),
        "diff_lambda": nrm((n_even, 4, DIFF_HEAD_DIM), 0.1),
        "diff_subln_g": gain((n_even, DIFF_V_DIM)),
        "gqa_w_qkv": nrm((n_odd, d, GQA_QKV), d ** -0.5),
        "gqa_w_out": nrm((n_odd, GQA_OUT, d), GQA_OUT ** -0.5),
        "gqa_q_norm_g": gain((n_odd, GQA_HEAD_DIM)),
        "gqa_k_norm_g": gain((n_odd, GQA_HEAD_DIM)),
        "final_norm_g": gain((d,)),
    }


def reference(x, c, ctx, c_ctx, mod_w, mod_b, norm1_g, norm2_g, router_w, router_b,
              expert_w_in, expert_b_in, expert_w_out, expert_b_out, ab_w_in, ab_w_out,
              conv_w, conv_b, conv_ln_g, conv_ln_b, diff_lambda, diff_subln_g,
              gqa_w_qkv, gqa_w_out, gqa_q_norm_g, gqa_k_norm_g, final_norm_g):
    b, n, d = x.shape
    n_ctx = ctx.shape[1]
    cos_d, sin_d = axial_rope(n, DIFF_HEAD_DIM, x.dtype)
    cos_g, sin_g = axial_rope(n, GQA_HEAD_DIM, x.dtype)
    s_lat = jax.nn.silu(c)
    s_ctx = jax.nn.silu(c_ctx)
    x_lat, x_ctx = x, ctx
    for layer in range(DEPTH):
        last = layer == DEPTH - 1
        mod_l = (s_lat @ mod_w[layer] + mod_b[layer])[:, None, :]
        mod_c = s_ctx @ mod_w[layer] + mod_b[layer]
        sh1_l, sc1_l, g1_l, sh2_l, sc2_l, g2_l = jnp.split(mod_l, 6, axis=-1)
        sh1_c, sc1_c, g1_c, sh2_c, sc2_c, g2_c = jnp.split(mod_c, 6, axis=-1)

        h_lat = modulate(x_lat, norm1_g[layer], sh1_l, sc1_l)
        h_ctx = modulate(x_ctx, norm1_g[layer], sh1_c, sc1_c)
        if layer % 2 == 0:
            i = layer // 2
            lam_init = 0.8 - 0.6 * math.exp(-0.3 * layer)
            o_ctx, o_lat = even_mixer(h_ctx, h_lat, ab_w_in[i], ab_w_out[i], conv_w[i], conv_b[i],
                                      conv_ln_g[i], conv_ln_b[i], diff_lambda[i], diff_subln_g[i],
                                      lam_init, cos_d, sin_d, not last)
        else:
            i = layer // 2
            o_ctx, o_lat = odd_mixer(h_ctx, h_lat, gqa_w_qkv[i], gqa_w_out[i], gqa_q_norm_g[i],
                                     gqa_k_norm_g[i], cos_g, sin_g, not last)
        x_lat = x_lat + g1_l * o_lat
        h2_lat = modulate(x_lat, norm2_g[layer], sh2_l, sc2_l)
        moe_args = (router_w[layer], router_b[layer], expert_w_in[layer], expert_b_in[layer],
                    expert_w_out[layer], expert_b_out[layer])
        if not last:
            x_ctx = x_ctx + g1_c * o_ctx
            h2_ctx = modulate(x_ctx, norm2_g[layer], sh2_c, sc2_c)
            tokens = jnp.concatenate([h2_ctx.reshape(-1, d), h2_lat.reshape(-1, d)], axis=0)
            f = moe(tokens, *moe_args)
            x_ctx = x_ctx + g2_c * f[:b * n_ctx].reshape(b, n_ctx, d)
            f_lat = f[b * n_ctx:].reshape(b, n, d)
        else:
            f_lat = moe(h2_lat.reshape(-1, d), *moe_args).reshape(b, n, d)
        x_lat = x_lat + g2_l * f_lat
    return rms_norm(x_lat, final_norm_g)
```

```python
import functools
import math

import jax
import jax.numpy as jnp
from jax import lax
from jax.experimental import pallas as pl
from jax.experimental.pallas import tpu as pltpu

GRID_W = 64
ROPE_THETA = 10000.0
RMS_EPS = 1e-6
LN_EPS = 1e-5
CONV_WIDTH = 31
CONV_PAD = CONV_WIDTH // 2
DIFF_HEAD_DIM = 64
GQA_HEAD_DIM = 128
GQA_REP = 4
TOP_K = 4
SWIGLU_LIMIT = 7.0
SWIGLU_ALPHA = 1.702

LANES = 128
SUBLANES = 8
VMEM_LIMIT_BYTES = 48 * 1024 * 1024
I32_1D_TILE = 1024

F32 = jnp.float32
BF16 = jnp.bfloat16
NEG_BIG = -1e30


def _params(*sem):
    return pltpu.CompilerParams(dimension_semantics=sem, vmem_limit_bytes=VMEM_LIMIT_BYTES)


def _tile(n, pref):
    t = min(n, pref)
    assert n % t == 0, (n, pref)
    return t


def _rms(x, eps):
    return x * lax.rsqrt(jnp.mean(x * x, axis=-1, keepdims=True) + eps)


def _mod_body(c_ref, w_ref, b_ref, o_ref):
    c = c_ref[...]
    s = c * jax.nn.sigmoid(c)
    o_ref[0] = jnp.dot(s, w_ref[0], precision=lax.Precision.HIGHEST,
                       preferred_element_type=F32) + b_ref[0]


def _modulation(c_all, mod_w, mod_b):
    depth, d, n = mod_w.shape
    rows = c_all.shape[0]
    tn = _tile(n, 1536)
    return pl.pallas_call(
        _mod_body,
        grid=(depth, n // tn),
        in_specs=[
            pl.BlockSpec((rows, d), lambda l, j: (0, 0)),
            pl.BlockSpec((1, d, tn), lambda l, j: (l, 0, j)),
            pl.BlockSpec((1, 1, tn), lambda l, j: (l, 0, j)),
        ],
        out_specs=pl.BlockSpec((1, rows, tn), lambda l, j: (l, 0, j)),
        out_shape=jax.ShapeDtypeStruct((depth, rows, n), F32),
        compiler_params=_params("parallel", "parallel"),
        name="modulation",
    )(c_all, mod_w, mod_b.reshape(depth, 1, n))


def _rope_tables(n, head_dim):
    quarter = head_dim // 4
    pos = jnp.arange(n, dtype=jnp.int32)
    row = (pos // GRID_W).astype(F32)
    col = (pos % GRID_W).astype(F32)
    inv = ROPE_THETA ** (-jnp.arange(quarter, dtype=F32) / quarter)
    lane = jnp.arange(LANES, dtype=jnp.int32) % head_dim
    axis = lane // (head_dim // 2)
    half = (lane % (head_dim // 2)) // quarter
    freq = inv[lane % quarter]
    p = jnp.where(axis[None, :] == 0, row[:, None], col[:, None])
    ang = p * freq[None, :]
    sign = jnp.where(half == 0, -1.0, 1.0).astype(F32)
    return jnp.cos(ang), jnp.sin(ang) * sign[None, :]


def _rope_chunk(x, cos, sin, quarter):
    lane = lax.broadcasted_iota(jnp.int32, x.shape, 1)
    first = (lane % (2 * quarter)) < quarter
    partner = jnp.where(first, pltpu.roll(x, LANES - quarter, 1), pltpu.roll(x, quarter, 1))
    return x * cos + partner * sin


def _proj_body(*refs, segs, rope, quarter):
    x_ref, mod_ref, g_ref, w_ref = refs[:4]
    pos = 4
    if rope:
        cos_ref, sin_ref = refs[pos:pos + 2]
        pos += 2
    n_gain = sum(1 for s in segs if s["gain"])
    gain_refs = refs[pos:pos + n_gain]
    out_refs = refs[pos + n_gain:]
    d = x_ref.shape[-1]

    x = x_ref[0]
    mod = mod_ref[0]
    shift, scale = mod[:, 0:d], mod[:, d:2 * d]
    h = _rms(x, RMS_EPS) * g_ref[...] * (1.0 + scale) + shift
    u = jnp.dot(h.astype(BF16), w_ref[...], preferred_element_type=F32)

    gi = 0
    for seg, o_ref in zip(segs, out_refs):
        start, width = seg["start"], seg["width"]
        gain = None
        if seg["gain"]:
            gain = gain_refs[gi][...]
            gi += 1
        if gain is None and not (seg["rope"] and rope) and seg["scale"] == 1.0:
            o_ref[0] = u[:, start:start + width].astype(o_ref.dtype)
            continue
        for j in range(width // LANES):
            v = u[:, start + j * LANES:start + (j + 1) * LANES]
            if gain is not None:
                v = _rms(v, RMS_EPS) * gain
            if seg["rope"] and rope:
                v = _rope_chunk(v, cos_ref[...], sin_ref[...], quarter)
            if seg["scale"] != 1.0:
                v = v * seg["scale"]
            o_ref[0, :, j * LANES:(j + 1) * LANES] = v.astype(o_ref.dtype)


def _project(x, mod, norm_g, w, segs, gains, tables, quarter, tm, name):
    b, l, d = x.shape
    n = w.shape[1]
    tm = _tile(l, tm)
    rope = tables is not None
    in_specs = [
        pl.BlockSpec((1, tm, d), lambda bi, i: (bi, i, 0)),
        pl.BlockSpec((1, 1, mod.shape[-1]), lambda bi, i: (bi, 0, 0)),
        pl.BlockSpec((1, d), lambda bi, i: (0, 0)),
        pl.BlockSpec((d, n), lambda bi, i: (0, 0)),
    ]
    args = [x, mod, norm_g.reshape(1, d), w]
    if rope:
        in_specs += [pl.BlockSpec((tm, LANES), lambda bi, i: (i, 0))] * 2
        args += list(tables)
    for g in gains:
        in_specs.append(pl.BlockSpec((1, LANES), lambda bi, i: (0, 0)))
        args.append(g.reshape(1, LANES))
    out_specs = [pl.BlockSpec((1, tm, s["width"]), lambda bi, i: (bi, i, 0)) for s in segs]
    out_shape = [jax.ShapeDtypeStruct((b, l, s["width"]), BF16) for s in segs]
    return pl.pallas_call(
        functools.partial(_proj_body, segs=segs, rope=rope, quarter=quarter),
        grid=(b, l // tm),
        in_specs=in_specs,
        out_specs=out_specs,
        out_shape=out_shape,
        compiler_params=_params("parallel", "parallel"),
        name=name,
    )(*args)


CONV_HALO = 16
CONV_ROWS = 32


def _conv_body(glu_ref, w_ref, b_ref, g_ref, beta_ref, o_ref, y_ref):
    l, ch = o_ref.shape[1], o_ref.shape[2]
    zeros = jnp.zeros((CONV_HALO, ch), F32)
    y_ref[0:CONV_HALO, :] = zeros
    y_ref[CONV_HALO + l:CONV_HALO + l + CONV_HALO, :] = zeros

    def glu_step(i, carry):
        r = pl.multiple_of(i * CONV_ROWS, CONV_ROWS)
        u = glu_ref[0, pl.ds(r, CONV_ROWS), :].astype(F32)
        y_ref[pl.ds(CONV_HALO + r, CONV_ROWS), :] = u[:, :ch] * jax.nn.sigmoid(u[:, ch:])
        return carry

    lax.fori_loop(0, l // CONV_ROWS, glu_step, 0)

    def conv_step(i, carry):
        r = pl.multiple_of(i * CONV_ROWS, CONV_ROWS)
        win = y_ref[pl.ds(r, CONV_ROWS + 2 * CONV_HALO), :]
        acc = jnp.zeros((CONV_ROWS, ch), F32)
        for s in range(SUBLANES):
            offs = [o for o in range(CONV_HALO - CONV_PAD, CONV_HALO - CONV_PAD + CONV_WIDTH)
                    if o % SUBLANES == s]
            span = max(offs) - s + CONV_ROWS
            shifted = win[s:s + span, :]
            for o in offs:
                j = o - (CONV_HALO - CONV_PAD)
                acc = acc + shifted[o - s:o - s + CONV_ROWS, :] * w_ref[j:j + 1, :]
        acc = acc + b_ref[...]
        mu = jnp.mean(acc, axis=-1, keepdims=True)
        cen = acc - mu
        var = jnp.mean(cen * cen, axis=-1, keepdims=True)
        z = cen * lax.rsqrt(var + LN_EPS) * g_ref[...] + beta_ref[...]
        o_ref[0, pl.ds(r, CONV_ROWS), :] = (z * jax.nn.sigmoid(z)).astype(o_ref.dtype)
        return carry

    lax.fori_loop(0, l // CONV_ROWS, conv_step, 0)


def _conv_branch(glu, conv_w, conv_b, ln_g, ln_b):
    b, l, two_ch = glu.shape
    ch = two_ch // 2
    assert l % CONV_ROWS == 0
    vec = lambda a: a.reshape(1, ch)
    const = lambda bi: (0, 0)
    return pl.pallas_call(
        _conv_body,
        grid=(b,),
        in_specs=[
            pl.BlockSpec((1, l, two_ch), lambda bi: (bi, 0, 0)),
            pl.BlockSpec((CONV_WIDTH, ch), const),
            pl.BlockSpec((1, ch), const),
            pl.BlockSpec((1, ch), const),
            pl.BlockSpec((1, ch), const),
        ],
        out_specs=pl.BlockSpec((1, l, ch), lambda bi: (bi, 0, 0)),
        out_shape=jax.ShapeDtypeStruct((b, l, ch), BF16),
        scratch_shapes=[pltpu.VMEM((l + 2 * CONV_HALO, ch), F32)],
        compiler_params=_params("parallel"),
        name="conv_branch",
    )(glu, conv_w, vec(conv_b), vec(ln_g), vec(ln_b))


def _attn_body(*refs, kind, n_seg, lam_init):
    q_ref = refs[0]
    k_refs = refs[1:1 + n_seg]
    v_refs = refs[1 + n_seg:1 + 2 * n_seg]
    pos = 1 + 2 * n_seg
    if kind == "diff":
        lam_ref, subg_ref = refs[pos:pos + 2]
        pos += 2
    o_ref = refs[pos]
    tq = q_ref.shape[1]

    q = q_ref[0]
    if kind == "diff":
        lane = lax.broadcasted_iota(jnp.int32, q.shape, 1)
        zero = jnp.zeros_like(q)
        rows = jnp.concatenate([jnp.where(lane < DIFF_HEAD_DIM, q, zero),
                                jnp.where(lane >= DIFF_HEAD_DIM, q, zero)], axis=0)
        n_stack = 2
    else:
        n_stack = q.shape[1] // LANES
        rows = jnp.concatenate([q[:, r * LANES:(r + 1) * LANES] for r in range(n_stack)], axis=0)

    nt = (((1,), (1,)), ((), ()))
    scores = [lax.dot_general(rows, k_ref[0], nt, preferred_element_type=F32) for k_ref in k_refs]
    m = functools.reduce(jnp.maximum, [jnp.max(s, axis=-1, keepdims=True) for s in scores])
    probs = [jnp.exp(s - m) for s in scores]
    denom = functools.reduce(lambda a, c: a + c, [jnp.sum(p, axis=-1, keepdims=True) for p in probs])
    acc = functools.reduce(
        lambda a, c: a + c,
        [jnp.dot(p.astype(BF16), v_ref[0], preferred_element_type=F32) for p, v_ref in zip(probs, v_refs)])
    out = acc * (1.0 / denom)

    if kind == "diff":
        lp = lam_ref[...]
        lam = (jnp.exp(jnp.sum(lp[0:1] * lp[1:2], axis=-1, keepdims=True))
               - jnp.exp(jnp.sum(lp[2:3] * lp[3:4], axis=-1, keepdims=True)) + lam_init)
        diff = out[0:tq] - lam * out[tq:2 * tq]
        o_ref[0] = (_rms(diff, RMS_EPS) * subg_ref[...] * (1.0 - lam_init)).astype(o_ref.dtype)
    else:
        for r in range(n_stack):
            o_ref[0, :, r * LANES:(r + 1) * LANES] = out[r * tq:(r + 1) * tq].astype(o_ref.dtype)


def _attention(q, ks, vs, kind, tq, extra=(), lam_init=0.0):
    b, lq, qtot = q.shape
    groups = ks[0].shape[-1] // LANES
    qw = qtot // groups
    tq = _tile(lq, tq)
    n_seg = len(ks)
    in_specs = [pl.BlockSpec((1, tq, qw), lambda bi, g, i: (bi, i, g))]
    for a in list(ks) + list(vs):
        in_specs.append(pl.BlockSpec((1, a.shape[1], LANES), lambda bi, g, i: (bi, 0, g)))
    for a in extra:
        in_specs.append(pl.BlockSpec(a.shape, lambda bi, g, i: (0, 0)))
    return pl.pallas_call(
        functools.partial(_attn_body, kind=kind, n_seg=n_seg, lam_init=lam_init),
        grid=(b, groups, lq // tq),
        in_specs=in_specs,
        out_specs=pl.BlockSpec((1, tq, qw), lambda bi, g, i: (bi, i, g)),
        out_shape=jax.ShapeDtypeStruct((b, lq, qtot), BF16),
        compiler_params=_params("parallel", "parallel", "parallel"),
        name="attention_" + kind,
    )(q, *ks, *vs, *extra)


def _outproj_body(*refs, n_act):
    act_refs = refs[:n_act]
    w_ref, x_ref, mod_ref, g_ref, rw_ref, rb_ref, xo_ref, h_ref, lg_ref = refs[n_act:]
    d = x_ref.shape[-1]
    acts = [a[0] for a in act_refs]
    cat = acts[0] if n_act == 1 else jnp.concatenate(acts, axis=-1)
    o = jnp.dot(cat, w_ref[...], preferred_element_type=F32)
    mod = mod_ref[0]
    gate1, shift2, scale2 = mod[:, 2 * d:3 * d], mod[:, 3 * d:4 * d], mod[:, 4 * d:5 * d]
    x_new = x_ref[0] + gate1 * o
    xo_ref[0] = x_new
    h = _rms(x_new, RMS_EPS) * g_ref[...] * (1.0 + scale2) + shift2
    h_ref[0] = h
    lg_ref[0] = jnp.dot(h.astype(BF16), rw_ref[...], preferred_element_type=F32) + rb_ref[...]


def _out_project(acts, w, x, mod, norm_g, router_w, router_b, tm):
    b, l, d = x.shape
    tm = _tile(l, tm)
    row = lambda bi, i: (bi, i, 0)
    const = lambda bi, i: (0, 0)
    in_specs = [pl.BlockSpec((1, tm, a.shape[-1]), row) for a in acts]
    in_specs += [
        pl.BlockSpec(w.shape, const),
        pl.BlockSpec((1, tm, d), row),
        pl.BlockSpec((1, 1, mod.shape[-1]), lambda bi, i: (bi, 0, 0)),
        pl.BlockSpec((1, d), const),
        pl.BlockSpec(router_w.shape, const),
        pl.BlockSpec((1, LANES), const),
    ]
    return pl.pallas_call(
        functools.partial(_outproj_body, n_act=len(acts)),
        grid=(b, l // tm),
        in_specs=in_specs,
        out_specs=[pl.BlockSpec((1, tm, d), row), pl.BlockSpec((1, tm, d), row),
                   pl.BlockSpec((1, tm, LANES), row)],
        out_shape=[jax.ShapeDtypeStruct((b, l, d), F32), jax.ShapeDtypeStruct((b, l, d), F32),
                   jax.ShapeDtypeStruct((b, l, LANES), F32)],
        compiler_params=_params("parallel", "parallel"),
        name="out_project",
    )(*acts, w, x, mod, norm_g.reshape(1, d), router_w, router_b)


def _router_body(lg_ref, out_ref, cnt_ref, carry_ref):
    @pl.when(pl.program_id(0) == 0)
    def _():
        carry_ref[...] = jnp.zeros_like(carry_ref)

    logits = lg_ref[...]
    tr = logits.shape[0]
    lane = lax.broadcasted_iota(jnp.int32, logits.shape, 1).astype(F32)
    work = logits
    vals, sels, ids = [], [], []
    for _ in range(TOP_K):
        m = jnp.max(work, axis=-1, keepdims=True)
        idx = jnp.min(jnp.where(work == m, lane, float(LANES)), axis=-1, keepdims=True)
        sel = lane == idx
        vals.append(m)
        ids.append(idx)
        sels.append(sel)
        work = jnp.where(sel, -jnp.inf, work)
    exps = [jnp.exp(v - vals[0]) for v in vals]
    inv = 1.0 / functools.reduce(lambda a, c: a + c, exps)

    onehot = functools.reduce(lambda a, c: a + c, [s.astype(F32) for s in sels])
    r_i = lax.broadcasted_iota(jnp.int32, (tr, tr), 0)
    c_i = lax.broadcasted_iota(jnp.int32, (tr, tr), 1)
    strict_lower = jnp.where(r_i > c_i, 1.0, 0.0).astype(BF16)
    before = jnp.dot(strict_lower, onehot.astype(BF16), preferred_element_type=F32) + carry_ref[...]

    out = jnp.zeros(logits.shape, F32)
    for k in range(TOP_K):
        rank = jnp.sum(jnp.where(sels[k], before, 0.0), axis=-1, keepdims=True)
        out = jnp.where(lane == k, ids[k], out)
        out = jnp.where(lane == TOP_K + k, exps[k] * inv, out)
        out = jnp.where(lane == 2 * TOP_K + k, rank, out)
    out_ref[...] = out
    carry_ref[...] = carry_ref[...] + jnp.sum(onehot, axis=0, keepdims=True)
    cnt_ref[...] = carry_ref[...]


def _route(logits):
    t = logits.shape[0]
    tr = _tile(t, 256)
    return pl.pallas_call(
        _router_body,
        grid=(t // tr,),
        in_specs=[pl.BlockSpec((tr, LANES), lambda i: (i, 0))],
        out_specs=[pl.BlockSpec((tr, LANES), lambda i: (i, 0)),
                   pl.BlockSpec((1, LANES), lambda i: (0, 0))],
        out_shape=[jax.ShapeDtypeStruct((t, LANES), F32), jax.ShapeDtypeStruct((1, LANES), F32)],
        scratch_shapes=[pltpu.VMEM((1, LANES), F32)],
        compiler_params=_params("arbitrary"),
        name="router",
    )(logits)


MOE_TOKENS = I32_1D_TILE // TOP_K


def _row_copy(src, src_row, dst, dst_row, sem):
    return pltpu.make_async_copy(src.at[pl.ds(src_row, 1), :], dst.at[pl.ds(dst_row, 1), :], sem)


def _load_slots(dest_hbm, idx_ref, idx_sem, step):
    start = pl.multiple_of(step * I32_1D_TILE, I32_1D_TILE)
    cp = pltpu.make_async_copy(dest_hbm.at[pl.ds(start, I32_1D_TILE)], idx_ref, idx_sem)
    cp.start()
    cp.wait()


def _dispatch_body(dest_hbm, h_ref, init_hbm, out_hbm, idx_ref, idx_sem, sem):
    del init_hbm
    _load_slots(dest_hbm, idx_ref, idx_sem, pl.program_id(0))

    def issue(r, carry):
        for k in range(TOP_K):
            _row_copy(h_ref, r, out_hbm, idx_ref[r * TOP_K + k], sem).start()
        return carry

    lax.fori_loop(0, MOE_TOKENS, issue, 0)

    def drain(r, carry):
        for k in range(TOP_K):
            _row_copy(h_ref, r, out_hbm, idx_ref[r * TOP_K + k], sem).wait()
        return carry

    lax.fori_loop(0, MOE_TOKENS, drain, 0)


def _dispatch(dest_flat, h, n_rows):
    t, d = h.shape
    assert t % MOE_TOKENS == 0
    init = jnp.zeros((n_rows, d), F32)
    return pl.pallas_call(
        _dispatch_body,
        grid=(t // MOE_TOKENS,),
        in_specs=[
            pl.BlockSpec(memory_space=pl.ANY),
            pl.BlockSpec((MOE_TOKENS, d), lambda i: (i, 0)),
            pl.BlockSpec(memory_space=pl.ANY),
        ],
        out_specs=pl.BlockSpec(memory_space=pl.ANY),
        out_shape=jax.ShapeDtypeStruct((n_rows, d), F32),
        scratch_shapes=[pltpu.SMEM((I32_1D_TILE,), jnp.int32), pltpu.SemaphoreType.DMA(()),
                        pltpu.SemaphoreType.DMA(())],
        input_output_aliases={2: 0},
        compiler_params=_params("arbitrary"),
        name="dispatch",
    )(dest_flat, h, init)


GROUP_ROWS = 256


def _ffn_body(te_ref, used_ref, x_ref, wi_ref, bi_ref, wo_ref, bo_ref, y_ref):
    i = pl.program_id(0)
    ff = wo_ref.shape[1]

    @pl.when(i < used_ref[0])
    def _():
        h = jnp.dot(x_ref[...].astype(BF16), wi_ref[0], preferred_element_type=F32) + bi_ref[0]
        g = jnp.minimum(h[:, :ff], SWIGLU_LIMIT)
        lin = jnp.clip(h[:, ff:], -SWIGLU_LIMIT, SWIGLU_LIMIT)
        act = g * jax.nn.sigmoid(SWIGLU_ALPHA * g) * (lin + 1.0)
        y_ref[...] = jnp.dot(act.astype(BF16), wo_ref[0], preferred_element_type=F32) + bo_ref[0]

    @pl.when(i >= used_ref[0])
    def _():
        y_ref[...] = jnp.zeros_like(y_ref)


def _expert_ffn(tile_expert, n_used, xs, w_in, b_in, w_out, b_out):
    p, d = xs.shape
    e, _, two_ff = w_in.shape
    ff = two_ff // 2
    n_tiles = p // GROUP_ROWS
    grid_spec = pltpu.PrefetchScalarGridSpec(
        num_scalar_prefetch=2,
        grid=(n_tiles,),
        in_specs=[
            pl.BlockSpec((GROUP_ROWS, d), lambda i, te, nu: (i, 0)),
            pl.BlockSpec((1, d, two_ff), lambda i, te, nu: (te[i], 0, 0)),
            pl.BlockSpec((1, 1, two_ff), lambda i, te, nu: (te[i], 0, 0)),
            pl.BlockSpec((1, ff, d), lambda i, te, nu: (te[i], 0, 0)),
            pl.BlockSpec((1, 1, d), lambda i, te, nu: (te[i], 0, 0)),
        ],
        out_specs=pl.BlockSpec((GROUP_ROWS, d), lambda i, te, nu: (i, 0)),
    )
    return pl.pallas_call(
        _ffn_body,
        grid_spec=grid_spec,
        out_shape=jax.ShapeDtypeStruct((p, d), F32),
        compiler_params=_params("arbitrary"),
        name="expert_ffn",
    )(tile_expert, n_used, xs, w_in, b_in.reshape(e, 1, two_ff), w_out, b_out.reshape(e, 1, d))


def _combine_body(*refs, final_norm):
    dest_hbm, route_ref, x_ref, mod_ref = refs[:4]
    pos = 4
    if final_norm:
        fg_ref = refs[pos]
        pos += 1
    y_hbm, o_ref, buf_ref, idx_ref, idx_sem, sem = refs[pos:]
    d = x_ref.shape[-1]
    step = pl.program_id(0) * pl.num_programs(1) + pl.program_id(1)
    _load_slots(dest_hbm, idx_ref, idx_sem, step)

    def issue(r, carry):
        for k in range(TOP_K):
            _row_copy(y_hbm, idx_ref[r * TOP_K + k], buf_ref.at[k], r, sem).start()
        return carry

    lax.fori_loop(0, MOE_TOKENS, issue, 0)

    def drain(r, carry):
        for k in range(TOP_K):
            _row_copy(y_hbm, idx_ref[r * TOP_K + k], buf_ref.at[k], r, sem).wait()
        return carry

    lax.fori_loop(0, MOE_TOKENS, drain, 0)

    route = route_ref[...]
    f = jnp.zeros((MOE_TOKENS, d), F32)
    for k in range(TOP_K):
        f = f + route[:, TOP_K + k:TOP_K + k + 1] * buf_ref[k]
    gate2 = mod_ref[0][:, 5 * d:6 * d]
    out = x_ref[0] + gate2 * f
    if final_norm:
        out = _rms(out, RMS_EPS) * fg_ref[...]
    o_ref[0] = out


def _combine(dest_flat, route, x, mod, y, final_g=None):
    b, l, d = x.shape
    assert l % MOE_TOKENS == 0
    nl = l // MOE_TOKENS
    final_norm = final_g is not None
    in_specs = [
        pl.BlockSpec(memory_space=pl.ANY),
        pl.BlockSpec((MOE_TOKENS, LANES), lambda bi, i: (bi * nl + i, 0)),
        pl.BlockSpec((1, MOE_TOKENS, d), lambda bi, i: (bi, i, 0)),
        pl.BlockSpec((1, 1, mod.shape[-1]), lambda bi, i: (bi, 0, 0)),
    ]
    args = [dest_flat, route, x, mod]
    if final_norm:
        in_specs.append(pl.BlockSpec((1, d), lambda bi, i: (0, 0)))
        args.append(final_g.reshape(1, d))
    in_specs.append(pl.BlockSpec(memory_space=pl.ANY))
    args.append(y)
    return pl.pallas_call(
        functools.partial(_combine_body, final_norm=final_norm),
        grid=(b, nl),
        in_specs=in_specs,
        out_specs=pl.BlockSpec((1, MOE_TOKENS, d), lambda bi, i: (bi, i, 0)),
        out_shape=jax.ShapeDtypeStruct((b, l, d), F32),
        scratch_shapes=[pltpu.VMEM((TOP_K, MOE_TOKENS, d), F32),
                        pltpu.SMEM((I32_1D_TILE,), jnp.int32),
                        pltpu.SemaphoreType.DMA(()), pltpu.SemaphoreType.DMA(())],
        compiler_params=_params("arbitrary", "arbitrary"),
        name="combine",
    )(*args)


def _moe_plan(logits, n_experts):
    t = logits.shape[0]
    route, counts = _route(logits)
    ids = route[:, 0:TOP_K].astype(jnp.int32)
    rank = route[:, 2 * TOP_K:3 * TOP_K].astype(jnp.int32)
    counts = counts[0, :n_experts].astype(jnp.int32)
    padded = ((counts + GROUP_ROWS - 1) // GROUP_ROWS) * GROUP_ROWS
    ends = jnp.cumsum(padded)
    starts = ends - padded
    dest = (starts[ids] + rank).reshape(-1)
    n_tiles = (t * TOP_K) // GROUP_ROWS + n_experts
    tile_start = jnp.arange(n_tiles, dtype=jnp.int32) * GROUP_ROWS
    tile_expert = jnp.sum((tile_start[:, None] >= ends[None, :]).astype(jnp.int32), axis=1)
    tile_expert = jnp.minimum(tile_expert, n_experts - 1)
    n_used = (ends[-1:] // GROUP_ROWS).astype(jnp.int32)
    return route, dest, tile_expert, n_used, n_tiles * GROUP_ROWS


def _pad_router(router_w, router_b):
    d, e = router_w.shape
    rw = jnp.zeros((d, LANES), BF16).at[:, :e].set(router_w.astype(BF16))
    rb = jnp.full((1, LANES), NEG_BIG, F32).at[0, :e].set(router_b)
    return rw, rb


def kernel(x, c, ctx, c_ctx, mod_w, mod_b, norm1_g, norm2_g, router_w, router_b, expert_w_in,
           expert_b_in, expert_w_out, expert_b_out, ab_w_in, ab_w_out, conv_w, conv_b, conv_ln_g,
           conv_ln_b, diff_lambda, diff_subln_g, gqa_w_qkv, gqa_w_out, gqa_q_norm_g, gqa_k_norm_g,
           final_norm_g):
    b, n, d = x.shape
    n_ctx = ctx.shape[1]
    n_experts = router_w.shape[-1]
    depth = mod_w.shape[0]
    assert depth == 2

    rows = -(-(b + 1) // SUBLANES) * SUBLANES
    c_all = jnp.zeros((rows, d), F32).at[:b].set(c).at[b].set(c_ctx)
    mod_all = _modulation(c_all, mod_w, mod_b)
    mod_lat = [mod_all[l, :b][:, None, :] for l in range(depth)]
    mod_ctx = [jnp.broadcast_to(mod_all[l, b][None, None, :], (b, 1, 6 * d)) for l in range(depth)]

    x_lat, x_ctx = x, ctx

    ch = d // 2
    diff_qk = (ab_w_in.shape[-1] - 2 * ch - ch) // 2
    segs0 = [
        dict(start=0, width=2 * ch, gain=False, rope=False, scale=1.0),
        dict(start=2 * ch, width=diff_qk, gain=False, rope=True, scale=DIFF_HEAD_DIM ** -0.5),
        dict(start=2 * ch + diff_qk, width=diff_qk, gain=False, rope=True, scale=1.0),
        dict(start=2 * ch + 2 * diff_qk, width=ch, gain=False, rope=False, scale=1.0),
    ]
    w_in0 = ab_w_in[0].astype(BF16)
    tables_d = _rope_tables(n, DIFF_HEAD_DIM)
    quarter_d = DIFF_HEAD_DIM // 4
    glu_c, q_c, k_c, v_c = _project(x_ctx, mod_ctx[0], norm1_g[0], w_in0, segs0, [], None, quarter_d,
                                    256, "project0_ctx")
    glu_l, q_l, k_l, v_l = _project(x_lat, mod_lat[0], norm1_g[0], w_in0, segs0, [], tables_d,
                                    quarter_d, 512, "project0_lat")
    conv_args = (conv_w[0], conv_b[0], conv_ln_g[0], conv_ln_b[0])
    a_c = _conv_branch(glu_c, *conv_args)
    a_l = _conv_branch(glu_l, *conv_args)
    lam_init = 0.8 - 0.6 * math.exp(-0.3 * 0)
    diff_extra = (diff_lambda[0], diff_subln_g[0].reshape(1, LANES))
    d_l = _attention(q_l, [k_c, k_l], [v_c, v_l], "diff", 256, diff_extra, lam_init)
    d_c = _attention(q_c, [k_c], [v_c], "diff", 256, diff_extra, lam_init)

    rw0, rb0 = _pad_router(router_w[0], router_b[0])
    w_out0 = ab_w_out[0].astype(BF16)
    x_ctx, h2_c, lg_c = _out_project([a_c, d_c], w_out0, x_ctx, mod_ctx[0], norm2_g[0], rw0, rb0, 256)
    x_lat, h2_l, lg_l = _out_project([a_l, d_l], w_out0, x_lat, mod_lat[0], norm2_g[0], rw0, rb0, 512)

    t_ctx = b * n_ctx
    tokens = jnp.concatenate([h2_c.reshape(t_ctx, d), h2_l.reshape(b * n, d)], axis=0)
    logits = jnp.concatenate([lg_c.reshape(t_ctx, LANES), lg_l.reshape(b * n, LANES)], axis=0)
    route, dest, tile_expert, n_used, n_rows = _moe_plan(logits, n_experts)
    xs = _dispatch(dest, tokens, n_rows)
    ys = _expert_ffn(tile_expert, n_used, xs, expert_w_in[0].astype(BF16), expert_b_in[0],
                     expert_w_out[0].astype(BF16), expert_b_out[0])
    x_ctx = _combine(dest[:t_ctx * TOP_K], route[:t_ctx], x_ctx, mod_ctx[0], ys)
    x_lat = _combine(dest[t_ctx * TOP_K:], route[t_ctx:], x_lat, mod_lat[0], ys)

    nq = gqa_w_out.shape[1]
    nk = (gqa_w_qkv.shape[-1] - nq) // 2
    segs1 = [
        dict(start=0, width=nq, gain=True, rope=True, scale=GQA_HEAD_DIM ** -0.5),
        dict(start=nq, width=nk, gain=True, rope=True, scale=1.0),
        dict(start=nq + nk, width=nk, gain=False, rope=False, scale=1.0),
    ]
    w_qkv = gqa_w_qkv[0].astype(BF16)
    gains1 = [gqa_q_norm_g[0], gqa_k_norm_g[0]]
    tables_g = _rope_tables(n, GQA_HEAD_DIM)
    quarter_g = GQA_HEAD_DIM // 4
    _, k_c, v_c = _project(x_ctx, mod_ctx[1], norm1_g[1], w_qkv, segs1, gains1, None, quarter_g,
                           256, "project1_ctx")
    q_l, k_l, v_l = _project(x_lat, mod_lat[1], norm1_g[1], w_qkv, segs1, gains1, tables_g,
                             quarter_g, 512, "project1_lat")
    o_l = _attention(q_l, [k_c, k_l], [v_c, v_l], "gqa", 128)
    rw1, rb1 = _pad_router(router_w[1], router_b[1])
    x_lat, h2_l, lg_l = _out_project([o_l], gqa_w_out[0].astype(BF16), x_lat, mod_lat[1], norm2_g[1],
                                     rw1, rb1, 512)
    tokens = h2_l.reshape(b * n, d)
    route, dest, tile_expert, n_used, n_rows = _moe_plan(lg_l.reshape(b * n, LANES), n_experts)
    xs = _dispatch(dest, tokens, n_rows)
    ys = _expert_ffn(tile_expert, n_used, xs, expert_w_in[1].astype(BF16), expert_b_in[1],
                     expert_w_out[1].astype(BF16), expert_b_out[1])
    return _combine(dest, route, x_lat, mod_lat[1], ys, final_g=final_norm_g)
```

```python
import functools
import math

import jax
import jax.numpy as jnp
from jax import lax
from jax.experimental import pallas as pl
from jax.experimental.pallas import tpu as pltpu

GRID_W = 64
ROPE_THETA = 10000.0
RMS_EPS = 1e-6
LN_EPS = 1e-5
CONV_WIDTH = 31
CONV_PAD = CONV_WIDTH // 2
DIFF_HEAD_DIM = 64
GQA_HEAD_DIM = 128
GQA_REP = 4
TOP_K = 4
SWIGLU_LIMIT = 7.0
SWIGLU_ALPHA = 1.702

LANES = 128
SUBLANES = 8
VMEM_LIMIT_BYTES = 48 * 1024 * 1024
FFN_VMEM_LIMIT_BYTES = 56 * 1024 * 1024
I32_1D_TILE = 1024

F32 = jnp.float32
BF16 = jnp.bfloat16
NEG_BIG = -1e30
LOG2_E = math.log2(math.e)


def _params(*sem):
    return pltpu.CompilerParams(dimension_semantics=sem, vmem_limit_bytes=VMEM_LIMIT_BYTES)


def _tile(n, pref):
    t = min(n, pref)
    assert n % t == 0, (n, pref)
    return t


def _rms(x, eps):
    return x * lax.rsqrt(jnp.mean(x * x, axis=-1, keepdims=True) + eps)


def _mod_body(c_ref, w_ref, b_ref, o_ref):
    c = c_ref[...]
    s = c * jax.nn.sigmoid(c)
    o_ref[0] = jnp.dot(s, w_ref[0], precision=lax.Precision.HIGHEST,
                       preferred_element_type=F32) + b_ref[0]


def _modulation(c_all, mod_w, mod_b):
    depth, d, n = mod_w.shape
    rows = c_all.shape[0]
    tn = _tile(n, 1536)
    return pl.pallas_call(
        _mod_body,
        grid=(depth, n // tn),
        in_specs=[
            pl.BlockSpec((rows, d), lambda l, j: (0, 0)),
            pl.BlockSpec((1, d, tn), lambda l, j: (l, 0, j)),
            pl.BlockSpec((1, 1, tn), lambda l, j: (l, 0, j)),
        ],
        out_specs=pl.BlockSpec((1, rows, tn), lambda l, j: (l, 0, j)),
        out_shape=jax.ShapeDtypeStruct((depth, rows, n), F32),
        compiler_params=_params("parallel", "parallel"),
        name="modulation",
    )(c_all, mod_w, mod_b.reshape(depth, 1, n))


def _rope_tables(n, head_dim):
    quarter = head_dim // 4
    pos = jnp.arange(n, dtype=jnp.int32)
    row = (pos // GRID_W).astype(F32)
    col = (pos % GRID_W).astype(F32)
    inv = ROPE_THETA ** (-jnp.arange(quarter, dtype=F32) / quarter)
    lane = jnp.arange(LANES, dtype=jnp.int32) % head_dim
    axis = lane // (head_dim // 2)
    half = (lane % (head_dim // 2)) // quarter
    freq = inv[lane % quarter]
    p = jnp.where(axis[None, :] == 0, row[:, None], col[:, None])
    ang = p * freq[None, :]
    sign = jnp.where(half == 0, -1.0, 1.0).astype(F32)
    return jnp.cos(ang), jnp.sin(ang) * sign[None, :]


def _rope_chunk(x, cos, sin, quarter):
    lane = lax.broadcasted_iota(jnp.int32, x.shape, 1)
    first = (lane % (2 * quarter)) < quarter
    partner = jnp.where(first, pltpu.roll(x, LANES - quarter, 1), pltpu.roll(x, quarter, 1))
    return x * cos + partner * sin


def _proj_body(*refs, segs, rope, quarter):
    x_ref, mod_ref, g_ref, w_ref = refs[:4]
    pos = 4
    if rope:
        cos_ref, sin_ref = refs[pos:pos + 2]
        pos += 2
    n_gain = sum(1 for s in segs if s["gain"])
    gain_refs = refs[pos:pos + n_gain]
    out_refs = refs[pos + n_gain:]
    d = x_ref.shape[-1]

    x = x_ref[0]
    mod = mod_ref[0]
    shift, scale = mod[:, 0:d], mod[:, d:2 * d]
    h = _rms(x, RMS_EPS) * g_ref[...] * (1.0 + scale) + shift
    u = jnp.dot(h.astype(BF16), w_ref[...], preferred_element_type=F32)

    gi = 0
    for seg, o_ref in zip(segs, out_refs):
        start, width = seg["start"], seg["width"]
        gain = None
        if seg["gain"]:
            gain = gain_refs[gi][...]
            gi += 1
        if gain is None and not (seg["rope"] and rope) and seg["scale"] == 1.0:
            o_ref[0] = u[:, start:start + width].astype(o_ref.dtype)
            continue
        for j in range(width // LANES):
            v = u[:, start + j * LANES:start + (j + 1) * LANES]
            if gain is not None:
                v = _rms(v, RMS_EPS) * gain
            if seg["rope"] and rope:
                v = _rope_chunk(v, cos_ref[...], sin_ref[...], quarter)
            if seg["scale"] != 1.0:
                v = v * seg["scale"]
            o_ref[0, :, j * LANES:(j + 1) * LANES] = v.astype(o_ref.dtype)


def _project(x, mod, norm_g, w, segs, gains, tables, quarter, tm, name):
    b, l, d = x.shape
    n = w.shape[1]
    tm = _tile(l, tm)
    rope = tables is not None
    in_specs = [
        pl.BlockSpec((1, tm, d), lambda bi, i: (bi, i, 0)),
        pl.BlockSpec((1, 1, mod.shape[-1]), lambda bi, i: (bi, 0, 0)),
        pl.BlockSpec((1, d), lambda bi, i: (0, 0)),
        pl.BlockSpec((d, n), lambda bi, i: (0, 0)),
    ]
    args = [x, mod, norm_g.reshape(1, d), w]
    if rope:
        in_specs += [pl.BlockSpec((tm, LANES), lambda bi, i: (i, 0))] * 2
        args += list(tables)
    for g in gains:
        in_specs.append(pl.BlockSpec((1, LANES), lambda bi, i: (0, 0)))
        args.append(g.reshape(1, LANES))
    out_specs = [pl.BlockSpec((1, tm, s["width"]), lambda bi, i: (bi, i, 0)) for s in segs]
    out_shape = [jax.ShapeDtypeStruct((b, l, s["width"]), BF16) for s in segs]
    return pl.pallas_call(
        functools.partial(_proj_body, segs=segs, rope=rope, quarter=quarter),
        grid=(b, l // tm),
        in_specs=in_specs,
        out_specs=out_specs,
        out_shape=out_shape,
        compiler_params=_params("parallel", "parallel"),
        name=name,
    )(*args)


CONV_HALO = 16
CONV_ROWS = 32


def _conv_body(glu_ref, w_ref, b_ref, g_ref, beta_ref, o_ref, y_ref):
    l, ch = o_ref.shape[1], o_ref.shape[2]
    zeros = jnp.zeros((CONV_HALO, ch), F32)
    y_ref[0:CONV_HALO, :] = zeros
    y_ref[CONV_HALO + l:CONV_HALO + l + CONV_HALO, :] = zeros

    def glu_step(i, carry):
        r = pl.multiple_of(i * CONV_ROWS, CONV_ROWS)
        u = glu_ref[0, pl.ds(r, CONV_ROWS), :].astype(F32)
        y_ref[pl.ds(CONV_HALO + r, CONV_ROWS), :] = u[:, :ch] * jax.nn.sigmoid(u[:, ch:])
        return carry

    lax.fori_loop(0, l // CONV_ROWS, glu_step, 0)

    def conv_step(i, carry):
        r = pl.multiple_of(i * CONV_ROWS, CONV_ROWS)
        win = y_ref[pl.ds(r, CONV_ROWS + 2 * CONV_HALO), :]
        acc = jnp.zeros((CONV_ROWS, ch), F32)
        for s in range(SUBLANES):
            offs = [o for o in range(CONV_HALO - CONV_PAD, CONV_HALO - CONV_PAD + CONV_WIDTH)
                    if o % SUBLANES == s]
            span = max(offs) - s + CONV_ROWS
            shifted = win[s:s + span, :]
            for o in offs:
                j = o - (CONV_HALO - CONV_PAD)
                acc = acc + shifted[o - s:o - s + CONV_ROWS, :] * w_ref[j:j + 1, :]
        acc = acc + b_ref[...]
        mu = jnp.mean(acc, axis=-1, keepdims=True)
        cen = acc - mu
        var = jnp.mean(cen * cen, axis=-1, keepdims=True)
        z = cen * lax.rsqrt(var + LN_EPS) * g_ref[...] + beta_ref[...]
        o_ref[0, pl.ds(r, CONV_ROWS), :] = (z * jax.nn.sigmoid(z)).astype(o_ref.dtype)
        return carry

    lax.fori_loop(0, l // CONV_ROWS, conv_step, 0)


def _conv_branch(glu, conv_w, conv_b, ln_g, ln_b):
    b, l, two_ch = glu.shape
    ch = two_ch // 2
    assert l % CONV_ROWS == 0
    vec = lambda a: a.reshape(1, ch)
    const = lambda bi: (0, 0)
    return pl.pallas_call(
        _conv_body,
        grid=(b,),
        in_specs=[
            pl.BlockSpec((1, l, two_ch), lambda bi: (bi, 0, 0)),
            pl.BlockSpec((CONV_WIDTH, ch), const),
            pl.BlockSpec((1, ch), const),
            pl.BlockSpec((1, ch), const),
            pl.BlockSpec((1, ch), const),
        ],
        out_specs=pl.BlockSpec((1, l, ch), lambda bi: (bi, 0, 0)),
        out_shape=jax.ShapeDtypeStruct((b, l, ch), BF16),
        scratch_shapes=[pltpu.VMEM((l + 2 * CONV_HALO, ch), F32)],
        compiler_params=_params("parallel"),
        name="conv_branch",
    )(glu, conv_w, vec(conv_b), vec(ln_g), vec(ln_b))


ATTN_SUB = 16
ATTN_KEYS = 256


def _attn_body(*refs, kind, n_seg, lam_init, tq):
    q_ref = refs[0]
    k_refs = refs[1:1 + n_seg]
    v_refs = refs[1 + n_seg:1 + 2 * n_seg]
    pos = 1 + 2 * n_seg
    if kind == "diff":
        lam_ref, subg_ref = refs[pos:pos + 2]
        pos += 2
    o_ref = refs[pos]
    s_refs, p_refs, linv_refs = refs[pos + 1:pos + 3], refs[pos + 3:pos + 5], refs[pos + 5:pos + 7]
    n_blk = q_ref.shape[1] // tq
    n_stack = 2 if kind == "diff" else q_ref.shape[2] // LANES
    rows_n = n_stack * tq
    chunks = []
    joint = 0
    for si, k_ref in enumerate(k_refs):
        assert k_ref.shape[1] % ATTN_KEYS == 0
        for off in range(0, k_ref.shape[1], ATTN_KEYS):
            chunks.append((si, off, joint))
            joint += ATTN_KEYS
    n_sub = rows_n // ATTN_SUB
    nt = (((1,), (1,)), ((), ()))

    def stacked_queries(i):
        q = q_ref[0, pl.ds(pl.multiple_of(i * tq, tq), tq), :]
        if kind == "diff":
            lane = lax.broadcasted_iota(jnp.int32, q.shape, 1)
            zero = jnp.zeros_like(q)
            return jnp.concatenate([jnp.where(lane < DIFF_HEAD_DIM, q, zero),
                                    jnp.where(lane >= DIFF_HEAD_DIM, q, zero)], axis=0)
        return jnp.concatenate([q[:, r * LANES:(r + 1) * LANES] for r in range(n_stack)], axis=0)

    def normalise(slot, j):
        sl = slice(j * ATTN_SUB, (j + 1) * ATTN_SUB)
        s = s_refs[slot][sl, :]
        p = jnp.exp2(s - jnp.max(s, axis=-1, keepdims=True))
        p_refs[slot][sl, :] = p.astype(BF16)
        inv = 1.0 / jnp.sum(p, axis=-1, keepdims=True)
        linv_refs[slot][sl, :] = jnp.broadcast_to(inv, (ATTN_SUB, LANES))

    def write_block(i, slot, acc):
        out = acc * linv_refs[slot][...]
        dst = pl.ds(pl.multiple_of(i * tq, tq), tq)
        if kind == "diff":
            lp = lam_ref[...]
            lam = (jnp.exp(jnp.sum(lp[0:1] * lp[1:2], axis=-1, keepdims=True))
                   - jnp.exp(jnp.sum(lp[2:3] * lp[3:4], axis=-1, keepdims=True)) + lam_init)
            diff = out[0:tq] - lam * out[tq:2 * tq]
            o_ref[0, dst, :] = (_rms(diff, RMS_EPS) * subg_ref[...] * (1.0 - lam_init)).astype(o_ref.dtype)
        else:
            for r in range(n_stack):
                o_ref[0, dst, r * LANES:(r + 1) * LANES] = out[r * tq:(r + 1) * tq].astype(o_ref.dtype)

    def stage(score_blk=None, score_slot=None, norm_slot=None, value_blk=None, value_slot=None):
        rows = None if score_blk is None else stacked_queries(score_blk)
        acc = None
        done = 0
        for c, (si, off, joint) in enumerate(chunks):
            if rows is not None:
                s_refs[score_slot][:, joint:joint + ATTN_KEYS] = lax.dot_general(
                    rows, k_refs[si][0, off:off + ATTN_KEYS, :], nt, preferred_element_type=F32)
            if value_blk is not None:
                part = jnp.dot(p_refs[value_slot][:, joint:joint + ATTN_KEYS],
                               v_refs[si][0, off:off + ATTN_KEYS, :], preferred_element_type=F32)
                acc = part if acc is None else acc + part
            if norm_slot is not None:
                upto = (n_sub * (c + 1)) // len(chunks)
                for j in range(done, upto):
                    normalise(norm_slot, j)
                done = upto
        if value_blk is not None:
            write_block(value_blk, value_slot, acc)

    stage(score_blk=0, score_slot=0)
    if n_blk == 1:
        stage(norm_slot=0)
        stage(value_blk=0, value_slot=0)
        return
    assert n_blk % 2 == 0
    stage(score_blk=1, score_slot=1, norm_slot=0)

    def pair(m, carry):
        i = 2 * m + 1
        stage(score_blk=i + 1, score_slot=0, norm_slot=1, value_blk=i - 1, value_slot=0)
        stage(score_blk=i + 2, score_slot=1, norm_slot=0, value_blk=i, value_slot=1)
        return carry

    lax.fori_loop(0, (n_blk - 2) // 2, pair, 0)
    stage(norm_slot=1, value_blk=n_blk - 2, value_slot=0)
    stage(value_blk=n_blk - 1, value_slot=1)


def _attention(q, ks, vs, kind, tq, extra=(), lam_init=0.0):
    b, lq, qtot = q.shape
    groups = ks[0].shape[-1] // LANES
    qw = qtot // groups
    tq = _tile(lq, tq)
    n_seg = len(ks)
    n_keys = sum(a.shape[1] for a in ks)
    rows_n = (2 if kind == "diff" else qw // LANES) * tq
    in_specs = [pl.BlockSpec((1, lq, qw), lambda bi, g: (bi, 0, g))]
    for a in list(ks) + list(vs):
        in_specs.append(pl.BlockSpec((1, a.shape[1], LANES), lambda bi, g: (bi, 0, g)))
    for a in extra:
        in_specs.append(pl.BlockSpec(a.shape, lambda bi, g: (0, 0)))
    return pl.pallas_call(
        functools.partial(_attn_body, kind=kind, n_seg=n_seg, lam_init=lam_init, tq=tq),
        grid=(b, groups),
        in_specs=in_specs,
        out_specs=pl.BlockSpec((1, lq, qw), lambda bi, g: (bi, 0, g)),
        out_shape=jax.ShapeDtypeStruct((b, lq, qtot), BF16),
        scratch_shapes=(2 * [pltpu.VMEM((rows_n, n_keys), F32)] + 2 * [pltpu.VMEM((rows_n, n_keys), BF16)]
                        + 2 * [pltpu.VMEM((rows_n, LANES), F32)]),
        compiler_params=_params("parallel", "parallel"),
        name="attention_" + kind,
    )(q, *ks, *vs, *extra)


def _outproj_body(*refs, n_act):
    act_refs = refs[:n_act]
    w_ref, x_ref, mod_ref, g_ref, rw_ref, rb_ref, xo_ref, h_ref, lg_ref = refs[n_act:]
    d = x_ref.shape[-1]
    acts = [a[0] for a in act_refs]
    cat = acts[0] if n_act == 1 else jnp.concatenate(acts, axis=-1)
    o = jnp.dot(cat, w_ref[...], preferred_element_type=F32)
    mod = mod_ref[0]
    gate1, shift2, scale2 = mod[:, 2 * d:3 * d], mod[:, 3 * d:4 * d], mod[:, 4 * d:5 * d]
    x_new = x_ref[0] + gate1 * o
    xo_ref[0] = x_new
    h = _rms(x_new, RMS_EPS) * g_ref[...] * (1.0 + scale2) + shift2
    h_ref[0] = h
    lg_ref[0] = jnp.dot(h.astype(BF16), rw_ref[...], preferred_element_type=F32) + rb_ref[...]


def _out_project(acts, w, x, mod, norm_g, router_w, router_b, tm):
    b, l, d = x.shape
    tm = _tile(l, tm)
    row = lambda bi, i: (bi, i, 0)
    const = lambda bi, i: (0, 0)
    in_specs = [pl.BlockSpec((1, tm, a.shape[-1]), row) for a in acts]
    in_specs += [
        pl.BlockSpec(w.shape, const),
        pl.BlockSpec((1, tm, d), row),
        pl.BlockSpec((1, 1, mod.shape[-1]), lambda bi, i: (bi, 0, 0)),
        pl.BlockSpec((1, d), const),
        pl.BlockSpec(router_w.shape, const),
        pl.BlockSpec((1, LANES), const),
    ]
    return pl.pallas_call(
        functools.partial(_outproj_body, n_act=len(acts)),
        grid=(b, l // tm),
        in_specs=in_specs,
        out_specs=[pl.BlockSpec((1, tm, d), row), pl.BlockSpec((1, tm, d), row),
                   pl.BlockSpec((1, tm, LANES), row)],
        out_shape=[jax.ShapeDtypeStruct((b, l, d), F32), jax.ShapeDtypeStruct((b, l, d), F32),
                   jax.ShapeDtypeStruct((b, l, LANES), F32)],
        compiler_params=_params("parallel", "parallel"),
        name="out_project",
    )(*acts, w, x, mod, norm_g.reshape(1, d), router_w, router_b)


def _router_body(lg_ref, out_ref, cnt_ref, carry_ref):
    @pl.when(pl.program_id(0) == 0)
    def _():
        carry_ref[...] = jnp.zeros_like(carry_ref)

    logits = lg_ref[...]
    tr = logits.shape[0]
    lane = lax.broadcasted_iota(jnp.int32, logits.shape, 1).astype(F32)
    work = logits
    vals, sels, ids = [], [], []
    for _ in range(TOP_K):
        m = jnp.max(work, axis=-1, keepdims=True)
        idx = jnp.min(jnp.where(work == m, lane, float(LANES)), axis=-1, keepdims=True)
        sel = lane == idx
        vals.append(m)
        ids.append(idx)
        sels.append(sel)
        work = jnp.where(sel, -jnp.inf, work)
    exps = [jnp.exp(v - vals[0]) for v in vals]
    inv = 1.0 / functools.reduce(lambda a, c: a + c, exps)

    onehot = functools.reduce(lambda a, c: a + c, [s.astype(F32) for s in sels])
    r_i = lax.broadcasted_iota(jnp.int32, (tr, tr), 0)
    c_i = lax.broadcasted_iota(jnp.int32, (tr, tr), 1)
    strict_lower = jnp.where(r_i > c_i, 1.0, 0.0).astype(BF16)
    before = jnp.dot(strict_lower, onehot.astype(BF16), preferred_element_type=F32) + carry_ref[...]

    out = jnp.zeros(logits.shape, F32)
    for k in range(TOP_K):
        rank = jnp.sum(jnp.where(sels[k], before, 0.0), axis=-1, keepdims=True)
        out = jnp.where(lane == k, ids[k], out)
        out = jnp.where(lane == TOP_K + k, exps[k] * inv, out)
        out = jnp.where(lane == 2 * TOP_K + k, rank, out)
    out_ref[...] = out
    carry_ref[...] = carry_ref[...] + jnp.sum(onehot, axis=0, keepdims=True)
    cnt_ref[...] = carry_ref[...]


def _route(logits):
    t = logits.shape[0]
    tr = _tile(t, 256)
    return pl.pallas_call(
        _router_body,
        grid=(t // tr,),
        in_specs=[pl.BlockSpec((tr, LANES), lambda i: (i, 0))],
        out_specs=[pl.BlockSpec((tr, LANES), lambda i: (i, 0)),
                   pl.BlockSpec((1, LANES), lambda i: (0, 0))],
        out_shape=[jax.ShapeDtypeStruct((t, LANES), F32), jax.ShapeDtypeStruct((1, LANES), F32)],
        scratch_shapes=[pltpu.VMEM((1, LANES), F32)],
        compiler_params=_params("arbitrary"),
        name="router",
    )(logits)


MOE_TOKENS = I32_1D_TILE // TOP_K
MOE_UNROLL = 8
GROUP_ROWS = 256


def _row_copy(src, src_row, dst, dst_row, sem):
    return pltpu.make_async_copy(src.at[pl.ds(src_row, 1), :], dst.at[pl.ds(dst_row, 1), :], sem)


def _load_slots(dest_hbm, idx_ref, idx_sem, step):
    start = pl.multiple_of(step * I32_1D_TILE, I32_1D_TILE)
    cp = pltpu.make_async_copy(dest_hbm.at[pl.ds(start, I32_1D_TILE)], idx_ref, idx_sem)
    cp.start()
    cp.wait()


def _issue_rows(make_copy):
    def issue(g, carry):
        for u in range(MOE_UNROLL):
            for k in range(TOP_K):
                make_copy(g * MOE_UNROLL + u, k).start(priority=k % 2)
        return carry

    lax.fori_loop(0, MOE_TOKENS // MOE_UNROLL, issue, 0)


def _wait_rows(all_rows_ref, sem):
    pltpu.make_async_copy(all_rows_ref, all_rows_ref, sem).wait()


def _dispatch_body(cnt_ref, start_ref, dest_hbm, h_ref, out_hbm, idx_ref, zero_ref, idx_sem, sem, pad_sem):
    step = pl.program_id(0)
    _load_slots(dest_hbm, idx_ref, idx_sem, step)
    _issue_rows(lambda r, k: _row_copy(h_ref, r, out_hbm, idx_ref[r * TOP_K + k], sem))
    _wait_rows(out_hbm.at[pl.ds(0, MOE_TOKENS * TOP_K), :], sem)

    @pl.when(step == pl.num_programs(0) - 1)
    def _():
        zero_ref[...] = jnp.zeros_like(zero_ref)

        def per_expert(e, carry):
            n = cnt_ref[e]
            n_pad = ((n + GROUP_ROWS - 1) // GROUP_ROWS) * GROUP_ROWS
            base = start_ref[e]

            def fill(r, c):
                _row_copy(zero_ref, 0, out_hbm, base + r, pad_sem).start()
                return c

            def drain(r, c):
                _row_copy(zero_ref, 0, out_hbm, base + r, pad_sem).wait()
                return c

            lax.fori_loop(n, n_pad, fill, 0)
            lax.fori_loop(n, n_pad, drain, 0)
            return carry

        lax.fori_loop(0, cnt_ref.shape[0], per_expert, 0)


def _dispatch(counts, starts, dest_flat, h, n_rows):
    t, d = h.shape
    assert t % MOE_TOKENS == 0 and n_rows >= MOE_TOKENS * TOP_K
    grid_spec = pltpu.PrefetchScalarGridSpec(
        num_scalar_prefetch=2,
        grid=(t // MOE_TOKENS,),
        in_specs=[
            pl.BlockSpec(memory_space=pl.ANY),
            pl.BlockSpec((MOE_TOKENS, d), lambda i, cnt, st: (i, 0)),
        ],
        out_specs=pl.BlockSpec(memory_space=pl.ANY),
        scratch_shapes=[pltpu.SMEM((I32_1D_TILE,), jnp.int32), pltpu.VMEM((SUBLANES, d), F32),
                        pltpu.SemaphoreType.DMA(()), pltpu.SemaphoreType.DMA(()),
                        pltpu.SemaphoreType.DMA(())],
    )
    return pl.pallas_call(
        _dispatch_body,
        grid_spec=grid_spec,
        out_shape=jax.ShapeDtypeStruct((n_rows, d), F32),
        compiler_params=_params("arbitrary"),
        name="dispatch",
    )(counts, starts, dest_flat, h)


def _ffn_body(te_ref, used_ref, x_ref, wi_ref, bi_ref, wo_ref, bo_ref, y_ref, wi_bf, wo_bf):
    i = pl.program_id(0)
    ff = wo_ref.shape[1]

    @pl.when((i == 0) | (te_ref[i] != te_ref[jnp.maximum(i - 1, 0)]))
    def _():
        wi_bf[...] = wi_ref[0].astype(BF16)
        wo_bf[...] = wo_ref[0].astype(BF16)

    @pl.when(i < used_ref[0])
    def _():
        h = jnp.dot(x_ref[...].astype(BF16), wi_bf[...], preferred_element_type=F32) + bi_ref[0]
        g = jnp.minimum(h[:, :ff], SWIGLU_LIMIT)
        lin = jnp.clip(h[:, ff:], -SWIGLU_LIMIT, SWIGLU_LIMIT)
        act = g * jax.nn.sigmoid(SWIGLU_ALPHA * g) * (lin + 1.0)
        y_ref[...] = jnp.dot(act.astype(BF16), wo_bf[...], preferred_element_type=F32) + bo_ref[0]

    @pl.when(i >= used_ref[0])
    def _():
        y_ref[...] = jnp.zeros_like(y_ref)


def _expert_ffn(tile_expert, n_used, xs, w_in, b_in, w_out, b_out):
    p, d = xs.shape
    e, _, two_ff = w_in.shape
    ff = two_ff // 2
    n_tiles = p // GROUP_ROWS
    grid_spec = pltpu.PrefetchScalarGridSpec(
        num_scalar_prefetch=2,
        grid=(n_tiles,),
        in_specs=[
            pl.BlockSpec((GROUP_ROWS, d), lambda i, te, nu: (jnp.minimum(i, nu[0] - 1), 0)),
            pl.BlockSpec((1, d, two_ff), lambda i, te, nu: (te[i], 0, 0)),
            pl.BlockSpec((1, 1, two_ff), lambda i, te, nu: (te[i], 0, 0)),
            pl.BlockSpec((1, ff, d), lambda i, te, nu: (te[i], 0, 0)),
            pl.BlockSpec((1, 1, d), lambda i, te, nu: (te[i], 0, 0)),
        ],
        out_specs=pl.BlockSpec((GROUP_ROWS, d), lambda i, te, nu: (i, 0)),
        scratch_shapes=[pltpu.VMEM((d, two_ff), BF16), pltpu.VMEM((ff, d), BF16)],
    )
    return pl.pallas_call(
        _ffn_body,
        grid_spec=grid_spec,
        out_shape=jax.ShapeDtypeStruct((p, d), F32),
        compiler_params=pltpu.CompilerParams(dimension_semantics=("arbitrary",),
                                             vmem_limit_bytes=FFN_VMEM_LIMIT_BYTES),
        name="expert_ffn",
    )(tile_expert, n_used, xs, w_in, b_in.reshape(e, 1, two_ff), w_out, b_out.reshape(e, 1, d))


def _combine_body(*refs, final_norm):
    dest_hbm, route_ref, x_ref, mod_ref = refs[:4]
    pos = 4
    if final_norm:
        fg_ref = refs[pos]
        pos += 1
    y_hbm, o_ref, buf_ref, idx_ref, idx_sem, sem = refs[pos:]
    d = x_ref.shape[-1]
    step = pl.program_id(0) * pl.num_programs(1) + pl.program_id(1)
    _load_slots(dest_hbm, idx_ref, idx_sem, step)

    _issue_rows(lambda r, k: _row_copy(y_hbm, idx_ref[r * TOP_K + k], buf_ref.at[k], r, sem))
    _wait_rows(buf_ref, sem)

    route = route_ref[...]
    f = jnp.zeros((MOE_TOKENS, d), F32)
    for k in range(TOP_K):
        f = f + route[:, TOP_K + k:TOP_K + k + 1] * buf_ref[k]
    gate2 = mod_ref[0][:, 5 * d:6 * d]
    out = x_ref[0] + gate2 * f
    if final_norm:
        out = _rms(out, RMS_EPS) * fg_ref[...]
    o_ref[0] = out


def _combine(dest_flat, route, x, mod, y, final_g=None):
    b, l, d = x.shape
    assert l % MOE_TOKENS == 0
    nl = l // MOE_TOKENS
    final_norm = final_g is not None
    in_specs = [
        pl.BlockSpec(memory_space=pl.ANY),
        pl.BlockSpec((MOE_TOKENS, LANES), lambda bi, i: (bi * nl + i, 0)),
        pl.BlockSpec((1, MOE_TOKENS, d), lambda bi, i: (bi, i, 0)),
        pl.BlockSpec((1, 1, mod.shape[-1]), lambda bi, i: (bi, 0, 0)),
    ]
    args = [dest_flat, route, x, mod]
    if final_norm:
        in_specs.append(pl.BlockSpec((1, d), lambda bi, i: (0, 0)))
        args.append(final_g.reshape(1, d))
    in_specs.append(pl.BlockSpec(memory_space=pl.ANY))
    args.append(y)
    return pl.pallas_call(
        functools.partial(_combine_body, final_norm=final_norm),
        grid=(b, nl),
        in_specs=in_specs,
        out_specs=pl.BlockSpec((1, MOE_TOKENS, d), lambda bi, i: (bi, i, 0)),
        out_shape=jax.ShapeDtypeStruct((b, l, d), F32),
        scratch_shapes=[pltpu.VMEM((TOP_K, MOE_TOKENS, d), F32),
                        pltpu.SMEM((I32_1D_TILE,), jnp.int32),
                        pltpu.SemaphoreType.DMA(()), pltpu.SemaphoreType.DMA(())],
        compiler_params=_params("arbitrary", "arbitrary"),
        name="combine",
    )(*args)


def _moe_plan(logits, n_experts):
    t = logits.shape[0]
    route, counts = _route(logits)
    ids = route[:, 0:TOP_K].astype(jnp.int32)
    rank = route[:, 2 * TOP_K:3 * TOP_K].astype(jnp.int32)
    counts = counts[0, :n_experts].astype(jnp.int32)
    padded = ((counts + GROUP_ROWS - 1) // GROUP_ROWS) * GROUP_ROWS
    ends = jnp.cumsum(padded)
    starts = ends - padded
    dest = (starts[ids] + rank).reshape(-1)
    n_tiles = (t * TOP_K) // GROUP_ROWS + n_experts
    tile_start = jnp.arange(n_tiles, dtype=jnp.int32) * GROUP_ROWS
    tile_expert = jnp.sum((tile_start[:, None] >= ends[None, :]).astype(jnp.int32), axis=1)
    tile_expert = jnp.minimum(tile_expert, n_experts - 1)
    n_used = (ends[-1:] // GROUP_ROWS).astype(jnp.int32)
    return route, dest, counts, starts, tile_expert, n_used, n_tiles * GROUP_ROWS


def _moe_experts(tokens, logits, w_in, b_in, w_out, b_out):
    route, dest, counts, starts, tile_expert, n_used, n_rows = _moe_plan(logits, w_in.shape[0])
    xs = _dispatch(counts, starts, dest, tokens, n_rows)
    ys = _expert_ffn(tile_expert, n_used, xs, w_in, b_in, w_out, b_out)
    return route, dest, ys


def _pad_router(router_w, router_b):
    d, e = router_w.shape
    rw = jnp.zeros((d, LANES), BF16).at[:, :e].set(router_w.astype(BF16))
    rb = jnp.full((1, LANES), NEG_BIG, F32).at[0, :e].set(router_b)
    return rw, rb


def kernel(x, c, ctx, c_ctx, mod_w, mod_b, norm1_g, norm2_g, router_w, router_b, expert_w_in,
           expert_b_in, expert_w_out, expert_b_out, ab_w_in, ab_w_out, conv_w, conv_b, conv_ln_g,
           conv_ln_b, diff_lambda, diff_subln_g, gqa_w_qkv, gqa_w_out, gqa_q_norm_g, gqa_k_norm_g,
           final_norm_g):
    b, n, d = x.shape
    n_ctx = ctx.shape[1]
    n_experts = router_w.shape[-1]
    depth = mod_w.shape[0]
    assert depth == 2

    rows = -(-(b + 1) // SUBLANES) * SUBLANES
    c_all = jnp.zeros((rows, d), F32).at[:b].set(c).at[b].set(c_ctx)
    mod_all = _modulation(c_all, mod_w, mod_b)
    mod_lat = [mod_all[l, :b][:, None, :] for l in range(depth)]
    mod_ctx = [jnp.broadcast_to(mod_all[l, b][None, None, :], (b, 1, 6 * d)) for l in range(depth)]

    x_lat, x_ctx = x, ctx

    ch = d // 2
    diff_qk = (ab_w_in.shape[-1] - 2 * ch - ch) // 2
    segs0 = [
        dict(start=0, width=2 * ch, gain=False, rope=False, scale=1.0),
        dict(start=2 * ch, width=diff_qk, gain=False, rope=True, scale=DIFF_HEAD_DIM ** -0.5 * LOG2_E),
        dict(start=2 * ch + diff_qk, width=diff_qk, gain=False, rope=True, scale=1.0),
        dict(start=2 * ch + 2 * diff_qk, width=ch, gain=False, rope=False, scale=1.0),
    ]
    w_in0 = ab_w_in[0].astype(BF16)
    tables_d = _rope_tables(n, DIFF_HEAD_DIM)
    quarter_d = DIFF_HEAD_DIM // 4
    glu_c, q_c, k_c, v_c = _project(x_ctx, mod_ctx[0], norm1_g[0], w_in0, segs0, [], None, quarter_d,
                                    256, "project0_ctx")
    glu_l, q_l, k_l, v_l = _project(x_lat, mod_lat[0], norm1_g[0], w_in0, segs0, [], tables_d,
                                    quarter_d, 512, "project0_lat")
    conv_args = (conv_w[0], conv_b[0], conv_ln_g[0], conv_ln_b[0])
    a_c = _conv_branch(glu_c, *conv_args)
    a_l = _conv_branch(glu_l, *conv_args)
    lam_init = 0.8 - 0.6 * math.exp(-0.3 * 0)
    diff_extra = (diff_lambda[0], diff_subln_g[0].reshape(1, LANES))
    d_l = _attention(q_l, [k_c, k_l], [v_c, v_l], "diff", 256, diff_extra, lam_init)
    d_c = _attention(q_c, [k_c], [v_c], "diff", 128, diff_extra, lam_init)

    rw0, rb0 = _pad_router(router_w[0], router_b[0])
    w_out0 = ab_w_out[0].astype(BF16)
    x_ctx, h2_c, lg_c = _out_project([a_c, d_c], w_out0, x_ctx, mod_ctx[0], norm2_g[0], rw0, rb0, 256)
    x_lat, h2_l, lg_l = _out_project([a_l, d_l], w_out0, x_lat, mod_lat[0], norm2_g[0], rw0, rb0, 512)

    t_ctx = b * n_ctx
    tokens = jnp.concatenate([h2_c.reshape(t_ctx, d), h2_l.reshape(b * n, d)], axis=0)
    logits = jnp.concatenate([lg_c.reshape(t_ctx, LANES), lg_l.reshape(b * n, LANES)], axis=0)
    route, dest, ys = _moe_experts(tokens, logits, expert_w_in[0], expert_b_in[0], expert_w_out[0],
                                   expert_b_out[0])
    x_ctx = _combine(dest[:t_ctx * TOP_K], route[:t_ctx], x_ctx, mod_ctx[0], ys)
    x_lat = _combine(dest[t_ctx * TOP_K:], route[t_ctx:], x_lat, mod_lat[0], ys)

    nq = gqa_w_out.shape[1]
    nk = (gqa_w_qkv.shape[-1] - nq) // 2
    segs1 = [
        dict(start=0, width=nq, gain=True, rope=True, scale=GQA_HEAD_DIM ** -0.5 * LOG2_E),
        dict(start=nq, width=nk, gain=True, rope=True, scale=1.0),
        dict(start=nq + nk, width=nk, gain=False, rope=False, scale=1.0),
    ]
    w_qkv = gqa_w_qkv[0].astype(BF16)
    gains1 = [gqa_q_norm_g[0], gqa_k_norm_g[0]]
    tables_g = _rope_tables(n, GQA_HEAD_DIM)
    quarter_g = GQA_HEAD_DIM // 4
    _, k_c, v_c = _project(x_ctx, mod_ctx[1], norm1_g[1], w_qkv, segs1, gains1, None, quarter_g,
                           256, "project1_ctx")
    q_l, k_l, v_l = _project(x_lat, mod_lat[1], norm1_g[1], w_qkv, segs1, gains1, tables_g,
                             quarter_g, 512, "project1_lat")
    o_l = _attention(q_l, [k_c, k_l], [v_c, v_l], "gqa", 128)
    rw1, rb1 = _pad_router(router_w[1], router_b[1])
    x_lat, h2_l, lg_l = _out_project([o_l], gqa_w_out[0].astype(BF16), x_lat, mod_lat[1], norm2_g[1],
                                     rw1, rb1, 512)
    tokens = h2_l.reshape(b * n, d)
    route, dest, ys = _moe_experts(tokens, lg_l.reshape(b * n, LANES), expert_w_in[1], expert_b_in[1],
                                   expert_w_out[1], expert_b_out[1])
    return _combine(dest, route, x_lat, mod_lat[1], ys, final_g=final_norm_g)
```

```python
import functools
import math

import jax
import jax.numpy as jnp
from jax import lax
from jax.experimental import pallas as pl
from jax.experimental.pallas import tpu as pltpu

GRID_W = 64
ROPE_THETA = 10000.0
RMS_EPS = 1e-6
LN_EPS = 1e-5
CONV_WIDTH = 31
CONV_PAD = CONV_WIDTH // 2
DIFF_HEAD_DIM = 64
GQA_HEAD_DIM = 128
GQA_REP = 4
TOP_K = 4
SWIGLU_LIMIT = 7.0
SWIGLU_ALPHA = 1.702

LANES = 128
SUBLANES = 8
VMEM_LIMIT_BYTES = 48 * 1024 * 1024
FFN_VMEM_LIMIT_BYTES = 56 * 1024 * 1024
I32_1D_TILE = 1024

F32 = jnp.float32
BF16 = jnp.bfloat16
NEG_BIG = -1e30
LOG2_E = math.log2(math.e)


def _params(*sem):
    return pltpu.CompilerParams(dimension_semantics=sem, vmem_limit_bytes=VMEM_LIMIT_BYTES)


def _tile(n, pref):
    t = min(n, pref)
    assert n % t == 0, (n, pref)
    return t


def _rms(x, eps):
    return x * lax.rsqrt(jnp.mean(x * x, axis=-1, keepdims=True) + eps)


def _mod_body(c_ref, w_ref, b_ref, o_ref):
    c = c_ref[...]
    s = c * jax.nn.sigmoid(c)
    o_ref[0] = jnp.dot(s, w_ref[0], precision=lax.Precision.HIGHEST,
                       preferred_element_type=F32) + b_ref[0]


def _modulation(c_all, mod_w, mod_b):
    depth, d, n = mod_w.shape
    rows = c_all.shape[0]
    tn = _tile(n, 1536)
    return pl.pallas_call(
        _mod_body,
        grid=(depth, n // tn),
        in_specs=[
            pl.BlockSpec((rows, d), lambda l, j: (0, 0)),
            pl.BlockSpec((1, d, tn), lambda l, j: (l, 0, j)),
            pl.BlockSpec((1, 1, tn), lambda l, j: (l, 0, j)),
        ],
        out_specs=pl.BlockSpec((1, rows, tn), lambda l, j: (l, 0, j)),
        out_shape=jax.ShapeDtypeStruct((depth, rows, n), F32),
        compiler_params=_params("parallel", "parallel"),
        name="modulation",
    )(c_all, mod_w, mod_b.reshape(depth, 1, n))


def _rope_tables(n, head_dim):
    quarter = head_dim // 4
    pos = jnp.arange(n, dtype=jnp.int32)
    row = (pos // GRID_W).astype(F32)
    col = (pos % GRID_W).astype(F32)
    inv = ROPE_THETA ** (-jnp.arange(quarter, dtype=F32) / quarter)
    lane = jnp.arange(LANES, dtype=jnp.int32) % head_dim
    axis = lane // (head_dim // 2)
    half = (lane % (head_dim // 2)) // quarter
    freq = inv[lane % quarter]
    p = jnp.where(axis[None, :] == 0, row[:, None], col[:, None])
    ang = p * freq[None, :]
    sign = jnp.where(half == 0, -1.0, 1.0).astype(F32)
    return jnp.cos(ang), jnp.sin(ang) * sign[None, :]


def _rope_chunk(x, cos, sin, quarter):
    lane = lax.broadcasted_iota(jnp.int32, x.shape, 1)
    first = (lane % (2 * quarter)) < quarter
    partner = jnp.where(first, pltpu.roll(x, LANES - quarter, 1), pltpu.roll(x, quarter, 1))
    return x * cos + partner * sin


def _proj_body(*refs, segs, rope, quarter):
    x_ref, mod_ref, g_ref, w_ref = refs[:4]
    pos = 4
    if rope:
        cos_ref, sin_ref = refs[pos:pos + 2]
        pos += 2
    n_gain = sum(1 for s in segs if s["gain"])
    gain_refs = refs[pos:pos + n_gain]
    out_refs = refs[pos + n_gain:]
    d = x_ref.shape[-1]

    x = x_ref[0]
    mod = mod_ref[0]
    shift, scale = mod[:, 0:d], mod[:, d:2 * d]
    h = _rms(x, RMS_EPS) * g_ref[...] * (1.0 + scale) + shift
    u = jnp.dot(h.astype(BF16), w_ref[...], preferred_element_type=F32)

    gi = 0
    for seg, o_ref in zip(segs, out_refs):
        start, width = seg["start"], seg["width"]
        gain = None
        if seg["gain"]:
            gain = gain_refs[gi][...]
            gi += 1
        if gain is None and not (seg["rope"] and rope) and seg["scale"] == 1.0:
            o_ref[0] = u[:, start:start + width].astype(o_ref.dtype)
            continue
        for j in range(width // LANES):
            v = u[:, start + j * LANES:start + (j + 1) * LANES]
            if gain is not None:
                v = _rms(v, RMS_EPS) * gain
            if seg["rope"] and rope:
                v = _rope_chunk(v, cos_ref[...], sin_ref[...], quarter)
            if seg["scale"] != 1.0:
                v = v * seg["scale"]
            o_ref[0, :, j * LANES:(j + 1) * LANES] = v.astype(o_ref.dtype)


def _project(x, mod, norm_g, w, segs, gains, tables, quarter, tm, name):
    b, l, d = x.shape
    n = w.shape[1]
    tm = _tile(l, tm)
    rope = tables is not None
    in_specs = [
        pl.BlockSpec((1, tm, d), lambda bi, i: (bi, i, 0)),
        pl.BlockSpec((1, 1, mod.shape[-1]), lambda bi, i: (bi, 0, 0)),
        pl.BlockSpec((1, d), lambda bi, i: (0, 0)),
        pl.BlockSpec((d, n), lambda bi, i: (0, 0)),
    ]
    args = [x, mod, norm_g.reshape(1, d), w]
    if rope:
        in_specs += [pl.BlockSpec((tm, LANES), lambda bi, i: (i, 0))] * 2
        args += list(tables)
    for g in gains:
        in_specs.append(pl.BlockSpec((1, LANES), lambda bi, i: (0, 0)))
        args.append(g.reshape(1, LANES))
    out_specs = [pl.BlockSpec((1, tm, s["width"]), lambda bi, i: (bi, i, 0)) for s in segs]
    out_shape = [jax.ShapeDtypeStruct((b, l, s["width"]), BF16) for s in segs]
    return pl.pallas_call(
        functools.partial(_proj_body, segs=segs, rope=rope, quarter=quarter),
        grid=(b, l // tm),
        in_specs=in_specs,
        out_specs=out_specs,
        out_shape=out_shape,
        compiler_params=_params("parallel", "parallel"),
        name=name,
    )(*args)


CONV_HALO = 16
CONV_ROWS = 32


def _conv_body(glu_ref, w_ref, b_ref, g_ref, beta_ref, o_ref, y_ref):
    l, ch = o_ref.shape[1], o_ref.shape[2]
    zeros = jnp.zeros((CONV_HALO, ch), F32)
    y_ref[0:CONV_HALO, :] = zeros
    y_ref[CONV_HALO + l:CONV_HALO + l + CONV_HALO, :] = zeros

    def glu_step(i, carry):
        r = pl.multiple_of(i * CONV_ROWS, CONV_ROWS)
        u = glu_ref[0, pl.ds(r, CONV_ROWS), :].astype(F32)
        y_ref[pl.ds(CONV_HALO + r, CONV_ROWS), :] = u[:, :ch] * jax.nn.sigmoid(u[:, ch:])
        return carry

    lax.fori_loop(0, l // CONV_ROWS, glu_step, 0)

    def conv_step(i, carry):
        r = pl.multiple_of(i * CONV_ROWS, CONV_ROWS)
        win = y_ref[pl.ds(r, CONV_ROWS + 2 * CONV_HALO), :]
        acc = jnp.zeros((CONV_ROWS, ch), F32)
        for s in range(SUBLANES):
            offs = [o for o in range(CONV_HALO - CONV_PAD, CONV_HALO - CONV_PAD + CONV_WIDTH)
                    if o % SUBLANES == s]
            span = max(offs) - s + CONV_ROWS
            shifted = win[s:s + span, :]
            for o in offs:
                j = o - (CONV_HALO - CONV_PAD)
                acc = acc + shifted[o - s:o - s + CONV_ROWS, :] * w_ref[j:j + 1, :]
        acc = acc + b_ref[...]
        mu = jnp.mean(acc, axis=-1, keepdims=True)
        cen = acc - mu
        var = jnp.mean(cen * cen, axis=-1, keepdims=True)
        z = cen * lax.rsqrt(var + LN_EPS) * g_ref[...] + beta_ref[...]
        o_ref[0, pl.ds(r, CONV_ROWS), :] = (z * jax.nn.sigmoid(z)).astype(o_ref.dtype)
        return carry

    lax.fori_loop(0, l // CONV_ROWS, conv_step, 0)


def _conv_branch(glu, conv_w, conv_b, ln_g, ln_b):
    b, l, two_ch = glu.shape
    ch = two_ch // 2
    assert l % CONV_ROWS == 0
    vec = lambda a: a.reshape(1, ch)
    const = lambda bi: (0, 0)
    return pl.pallas_call(
        _conv_body,
        grid=(b,),
        in_specs=[
            pl.BlockSpec((1, l, two_ch), lambda bi: (bi, 0, 0)),
            pl.BlockSpec((CONV_WIDTH, ch), const),
            pl.BlockSpec((1, ch), const),
            pl.BlockSpec((1, ch), const),
            pl.BlockSpec((1, ch), const),
        ],
        out_specs=pl.BlockSpec((1, l, ch), lambda bi: (bi, 0, 0)),
        out_shape=jax.ShapeDtypeStruct((b, l, ch), BF16),
        scratch_shapes=[pltpu.VMEM((l + 2 * CONV_HALO, ch), F32)],
        compiler_params=_params("parallel"),
        name="conv_branch",
    )(glu, conv_w, vec(conv_b), vec(ln_g), vec(ln_b))


ATTN_SUB = 16
ATTN_KEYS = 256


def _attn_body(*refs, kind, n_seg, lam_init, tq):
    q_ref = refs[0]
    k_refs = refs[1:1 + n_seg]
    v_refs = refs[1 + n_seg:1 + 2 * n_seg]
    pos = 1 + 2 * n_seg
    if kind == "diff":
        lam_ref, subg_ref = refs[pos:pos + 2]
        pos += 2
    o_ref = refs[pos]
    s_refs, p_refs, linv_refs = refs[pos + 1:pos + 3], refs[pos + 3:pos + 5], refs[pos + 5:pos + 7]
    n_blk = q_ref.shape[1] // tq
    n_stack = 2 if kind == "diff" else q_ref.shape[2] // LANES
    rows_n = n_stack * tq
    chunks = []
    joint = 0
    for si, k_ref in enumerate(k_refs):
        assert k_ref.shape[1] % ATTN_KEYS == 0
        for off in range(0, k_ref.shape[1], ATTN_KEYS):
            chunks.append((si, off, joint))
            joint += ATTN_KEYS
    n_sub = rows_n // ATTN_SUB
    nt = (((1,), (1,)), ((), ()))

    def stacked_queries(i):
        q = q_ref[0, pl.ds(pl.multiple_of(i * tq, tq), tq), :]
        if kind == "diff":
            lane = lax.broadcasted_iota(jnp.int32, q.shape, 1)
            zero = jnp.zeros_like(q)
            return jnp.concatenate([jnp.where(lane < DIFF_HEAD_DIM, q, zero),
                                    jnp.where(lane >= DIFF_HEAD_DIM, q, zero)], axis=0)
        return jnp.concatenate([q[:, r * LANES:(r + 1) * LANES] for r in range(n_stack)], axis=0)

    def normalise(slot, j):
        sl = slice(j * ATTN_SUB, (j + 1) * ATTN_SUB)
        s = s_refs[slot][sl, :]
        p = jnp.exp2(s - jnp.max(s, axis=-1, keepdims=True))
        p_refs[slot][sl, :] = p.astype(BF16)
        inv = 1.0 / jnp.sum(p, axis=-1, keepdims=True)
        linv_refs[slot][sl, :] = jnp.broadcast_to(inv, (ATTN_SUB, LANES))

    def write_block(i, slot, acc):
        out = acc * linv_refs[slot][...]
        dst = pl.ds(pl.multiple_of(i * tq, tq), tq)
        if kind == "diff":
            lp = lam_ref[...]
            lam = (jnp.exp(jnp.sum(lp[0:1] * lp[1:2], axis=-1, keepdims=True))
                   - jnp.exp(jnp.sum(lp[2:3] * lp[3:4], axis=-1, keepdims=True)) + lam_init)
            diff = out[0:tq] - lam * out[tq:2 * tq]
            o_ref[0, dst, :] = (_rms(diff, RMS_EPS) * subg_ref[...] * (1.0 - lam_init)).astype(o_ref.dtype)
        else:
            for r in range(n_stack):
                o_ref[0, dst, r * LANES:(r + 1) * LANES] = out[r * tq:(r + 1) * tq].astype(o_ref.dtype)

    def stage(score_blk=None, score_slot=None, norm_slot=None, value_blk=None, value_slot=None):
        rows = None if score_blk is None else stacked_queries(score_blk)
        acc = None
        done = 0
        for c, (si, off, joint) in enumerate(chunks):
            if rows is not None:
                s_refs[score_slot][:, joint:joint + ATTN_KEYS] = lax.dot_general(
                    rows, k_refs[si][0, off:off + ATTN_KEYS, :], nt, preferred_element_type=F32)
            if value_blk is not None:
                part = jnp.dot(p_refs[value_slot][:, joint:joint + ATTN_KEYS],
                               v_refs[si][0, off:off + ATTN_KEYS, :], preferred_element_type=F32)
                acc = part if acc is None else acc + part
            if norm_slot is not None:
                upto = (n_sub * (c + 1)) // len(chunks)
                for j in range(done, upto):
                    normalise(norm_slot, j)
                done = upto
        if value_blk is not None:
            write_block(value_blk, value_slot, acc)

    stage(score_blk=0, score_slot=0)
    if n_blk == 1:
        stage(norm_slot=0)
        stage(value_blk=0, value_slot=0)
        return
    assert n_blk % 2 == 0
    stage(score_blk=1, score_slot=1, norm_slot=0)

    def pair(m, carry):
        i = 2 * m + 1
        stage(score_blk=i + 1, score_slot=0, norm_slot=1, value_blk=i - 1, value_slot=0)
        stage(score_blk=i + 2, score_slot=1, norm_slot=0, value_blk=i, value_slot=1)
        return carry

    lax.fori_loop(0, (n_blk - 2) // 2, pair, 0)
    stage(norm_slot=1, value_blk=n_blk - 2, value_slot=0)
    stage(value_blk=n_blk - 1, value_slot=1)


def _attention(q, ks, vs, kind, tq, extra=(), lam_init=0.0):
    b, lq, qtot = q.shape
    groups = ks[0].shape[-1] // LANES
    qw = qtot // groups
    tq = _tile(lq, tq)
    n_seg = len(ks)
    n_keys = sum(a.shape[1] for a in ks)
    rows_n = (2 if kind == "diff" else qw // LANES) * tq
    in_specs = [pl.BlockSpec((1, lq, qw), lambda bi, g: (bi, 0, g))]
    for a in list(ks) + list(vs):
        in_specs.append(pl.BlockSpec((1, a.shape[1], LANES), lambda bi, g: (bi, 0, g)))
    for a in extra:
        in_specs.append(pl.BlockSpec(a.shape, lambda bi, g: (0, 0)))
    return pl.pallas_call(
        functools.partial(_attn_body, kind=kind, n_seg=n_seg, lam_init=lam_init, tq=tq),
        grid=(b, groups),
        in_specs=in_specs,
        out_specs=pl.BlockSpec((1, lq, qw), lambda bi, g: (bi, 0, g)),
        out_shape=jax.ShapeDtypeStruct((b, lq, qtot), BF16),
        scratch_shapes=(2 * [pltpu.VMEM((rows_n, n_keys), F32)] + 2 * [pltpu.VMEM((rows_n, n_keys), BF16)]
                        + 2 * [pltpu.VMEM((rows_n, LANES), F32)]),
        compiler_params=_params("parallel", "parallel"),
        name="attention_" + kind,
    )(q, *ks, *vs, *extra)


def _outproj_body(*refs, n_act):
    act_refs = refs[:n_act]
    w_ref, x_ref, mod_ref, g_ref, rw_ref, rb_ref, xo_ref, h_ref, lg_ref = refs[n_act:]
    d = x_ref.shape[-1]
    acts = [a[0] for a in act_refs]
    cat = acts[0] if n_act == 1 else jnp.concatenate(acts, axis=-1)
    o = jnp.dot(cat, w_ref[...], preferred_element_type=F32)
    mod = mod_ref[0]
    gate1, shift2, scale2 = mod[:, 2 * d:3 * d], mod[:, 3 * d:4 * d], mod[:, 4 * d:5 * d]
    x_new = x_ref[0] + gate1 * o
    xo_ref[0] = x_new
    h = _rms(x_new, RMS_EPS) * g_ref[...] * (1.0 + scale2) + shift2
    h_ref[0] = h
    lg_ref[0] = jnp.dot(h.astype(BF16), rw_ref[...], preferred_element_type=F32) + rb_ref[...]


def _out_project(acts, w, x, mod, norm_g, router_w, router_b, tm):
    b, l, d = x.shape
    tm = _tile(l, tm)
    row = lambda bi, i: (bi, i, 0)
    const = lambda bi, i: (0, 0)
    in_specs = [pl.BlockSpec((1, tm, a.shape[-1]), row) for a in acts]
    in_specs += [
        pl.BlockSpec(w.shape, const),
        pl.BlockSpec((1, tm, d), row),
        pl.BlockSpec((1, 1, mod.shape[-1]), lambda bi, i: (bi, 0, 0)),
        pl.BlockSpec((1, d), const),
        pl.BlockSpec(router_w.shape, const),
        pl.BlockSpec((1, LANES), const),
    ]
    return pl.pallas_call(
        functools.partial(_outproj_body, n_act=len(acts)),
        grid=(b, l // tm),
        in_specs=in_specs,
        out_specs=[pl.BlockSpec((1, tm, d), row), pl.BlockSpec((1, tm, d), row),
                   pl.BlockSpec((1, tm, LANES), row)],
        out_shape=[jax.ShapeDtypeStruct((b, l, d), F32), jax.ShapeDtypeStruct((b, l, d), F32),
                   jax.ShapeDtypeStruct((b, l, LANES), F32)],
        compiler_params=_params("parallel", "parallel"),
        name="out_project",
    )(*acts, w, x, mod, norm_g.reshape(1, d), router_w, router_b)


def _router_body(lg_ref, out_ref, cnt_ref, carry_ref):
    @pl.when(pl.program_id(0) == 0)
    def _():
        carry_ref[...] = jnp.zeros_like(carry_ref)

    logits = lg_ref[...]
    tr = logits.shape[0]
    lane = lax.broadcasted_iota(jnp.int32, logits.shape, 1).astype(F32)
    work = logits
    vals, sels, ids = [], [], []
    for _ in range(TOP_K):
        m = jnp.max(work, axis=-1, keepdims=True)
        idx = jnp.min(jnp.where(work == m, lane, float(LANES)), axis=-1, keepdims=True)
        sel = lane == idx
        vals.append(m)
        ids.append(idx)
        sels.append(sel)
        work = jnp.where(sel, -jnp.inf, work)
    exps = [jnp.exp(v - vals[0]) for v in vals]
    inv = 1.0 / functools.reduce(lambda a, c: a + c, exps)

    onehot = functools.reduce(lambda a, c: a + c, [s.astype(F32) for s in sels])
    r_i = lax.broadcasted_iota(jnp.int32, (tr, tr), 0)
    c_i = lax.broadcasted_iota(jnp.int32, (tr, tr), 1)
    strict_lower = jnp.where(r_i > c_i, 1.0, 0.0).astype(BF16)
    before = jnp.dot(strict_lower, onehot.astype(BF16), preferred_element_type=F32) + carry_ref[...]

    out = jnp.zeros(logits.shape, F32)
    for k in range(TOP_K):
        rank = jnp.sum(jnp.where(sels[k], before, 0.0), axis=-1, keepdims=True)
        out = jnp.where(lane == k, ids[k], out)
        out = jnp.where(lane == TOP_K + k, exps[k] * inv, out)
        out = jnp.where(lane == 2 * TOP_K + k, rank, out)
    out_ref[...] = out
    carry_ref[...] = carry_ref[...] + jnp.sum(onehot, axis=0, keepdims=True)
    cnt_ref[...] = carry_ref[...]


def _route(logits):
    t = logits.shape[0]
    tr = _tile(t, 256)
    return pl.pallas_call(
        _router_body,
        grid=(t // tr,),
        in_specs=[pl.BlockSpec((tr, LANES), lambda i: (i, 0))],
        out_specs=[pl.BlockSpec((tr, LANES), lambda i: (i, 0)),
                   pl.BlockSpec((1, LANES), lambda i: (0, 0))],
        out_shape=[jax.ShapeDtypeStruct((t, LANES), F32), jax.ShapeDtypeStruct((1, LANES), F32)],
        scratch_shapes=[pltpu.VMEM((1, LANES), F32)],
        compiler_params=_params("arbitrary"),
        name="router",
    )(logits)


MOE_STEP_TOKENS = (1024, 512, 256)
MOE_UNROLL = 8
GROUP_ROWS = 256
assert all((n * TOP_K) % I32_1D_TILE == 0 for n in MOE_STEP_TOKENS)


def _step_tokens(n):
    return next(s for s in MOE_STEP_TOKENS if n % s == 0)


def _row_copy(src, src_row, dst, dst_row, sem):
    return pltpu.make_async_copy(src.at[pl.ds(src_row, 1), :], dst.at[pl.ds(dst_row, 1), :], sem)


def _load_slots(dest_hbm, idx_ref, idx_sem, step):
    n = idx_ref.shape[0]
    start = pl.multiple_of(step * n, n)
    cp = pltpu.make_async_copy(dest_hbm.at[pl.ds(start, n)], idx_ref, idx_sem)
    cp.start()
    cp.wait()


def _issue_rows(make_copy, tokens):
    def issue(g, carry):
        for u in range(MOE_UNROLL):
            for k in range(TOP_K):
                make_copy(g * MOE_UNROLL + u, k).start(priority=k % 2)
        return carry

    lax.fori_loop(0, tokens // MOE_UNROLL, issue, 0)


def _wait_rows(all_rows_ref, sem):
    pltpu.make_async_copy(all_rows_ref, all_rows_ref, sem).wait()


def _dispatch_body(cnt_ref, start_ref, dest_hbm, h_ref, out_hbm, idx_ref, zero_ref, idx_sem, sem, pad_sem):
    step = pl.program_id(0)
    tokens = h_ref.shape[0]
    _load_slots(dest_hbm, idx_ref, idx_sem, step)
    _issue_rows(lambda r, k: _row_copy(h_ref, r, out_hbm, idx_ref[r * TOP_K + k], sem), tokens)
    _wait_rows(out_hbm.at[pl.ds(0, tokens * TOP_K), :], sem)

    @pl.when(step == pl.num_programs(0) - 1)
    def _():
        zero_ref[...] = jnp.zeros_like(zero_ref)
        n_experts = cnt_ref.shape[0]

        def padded(e):
            return ((cnt_ref[e] + GROUP_ROWS - 1) // GROUP_ROWS) * GROUP_ROWS

        def per_expert(e, carry):
            n = cnt_ref[e]
            n_pad = padded(e)
            base = start_ref[e]

            def fill(r, c):
                _row_copy(zero_ref, 0, out_hbm, base + r, pad_sem).start()
                return c

            def drain(r, c):
                _row_copy(zero_ref, 0, out_hbm, base + r, pad_sem).wait()
                return c

            lax.fori_loop(n, n_pad, fill, 0)
            lax.fori_loop(n, n_pad, drain, 0)
            return carry

        lax.fori_loop(0, n_experts, per_expert, 0)

        first_free = (start_ref[n_experts - 1] + padded(n_experts - 1)) // GROUP_ROWS

        def tile_copy(j):
            rows = pl.ds(pl.multiple_of(j * GROUP_ROWS, GROUP_ROWS), GROUP_ROWS)
            return pltpu.make_async_copy(zero_ref, out_hbm.at[rows, :], pad_sem)

        def fill_tile(j, c):
            tile_copy(j).start()
            return c

        def drain_tile(j, c):
            tile_copy(j).wait()
            return c

        n_tiles = out_hbm.shape[0] // GROUP_ROWS
        lax.fori_loop(first_free, n_tiles, fill_tile, 0)
        lax.fori_loop(first_free, n_tiles, drain_tile, 0)


def _dispatch(counts, starts, dest_flat, h, n_rows):
    t, d = h.shape
    tokens = _step_tokens(t)
    assert n_rows >= tokens * TOP_K and n_rows % GROUP_ROWS == 0
    grid_spec = pltpu.PrefetchScalarGridSpec(
        num_scalar_prefetch=2,
        grid=(t // tokens,),
        in_specs=[
            pl.BlockSpec(memory_space=pl.ANY),
            pl.BlockSpec((tokens, d), lambda i, cnt, st: (i, 0)),
        ],
        out_specs=pl.BlockSpec(memory_space=pl.ANY),
        scratch_shapes=[pltpu.SMEM((tokens * TOP_K,), jnp.int32), pltpu.VMEM((GROUP_ROWS, d), F32),
                        pltpu.SemaphoreType.DMA(()), pltpu.SemaphoreType.DMA(()),
                        pltpu.SemaphoreType.DMA(())],
    )
    return pl.pallas_call(
        _dispatch_body,
        grid_spec=grid_spec,
        out_shape=jax.ShapeDtypeStruct((n_rows, d), F32),
        compiler_params=_params("arbitrary"),
        name="dispatch",
    )(counts, starts, dest_flat, h)


def _ffn_body(te_ref, used_ref, x_ref, wi_ref, bi_ref, wo_ref, bo_ref, y_ref, wi_bf, wo_bf):
    i = pl.program_id(0)
    ff = wo_ref.shape[1]

    @pl.when((i == 0) | (te_ref[i] != te_ref[jnp.maximum(i - 1, 0)]))
    def _():
        wi_bf[...] = wi_ref[0].astype(BF16)
        wo_bf[...] = wo_ref[0].astype(BF16)

    @pl.when(i < used_ref[0])
    def _():
        h = jnp.dot(x_ref[...].astype(BF16), wi_bf[...], preferred_element_type=F32) + bi_ref[0]
        g = jnp.minimum(h[:, :ff], SWIGLU_LIMIT)
        lin = jnp.clip(h[:, ff:], -SWIGLU_LIMIT, SWIGLU_LIMIT)
        act = g * jax.nn.sigmoid(SWIGLU_ALPHA * g) * (lin + 1.0)
        y_ref[...] = jnp.dot(act.astype(BF16), wo_bf[...], preferred_element_type=F32) + bo_ref[0]

    @pl.when(i >= used_ref[0])
    def _():
        y_ref[...] = jnp.zeros_like(y_ref)


def _expert_ffn(tile_expert, n_used, xs, w_in, b_in, w_out, b_out, layer):
    p, d = xs.shape
    depth, e, _, two_ff = w_in.shape
    ff = two_ff // 2
    n_tiles = p // GROUP_ROWS
    sq = pl.Squeezed()
    grid_spec = pltpu.PrefetchScalarGridSpec(
        num_scalar_prefetch=2,
        grid=(n_tiles,),
        in_specs=[
            pl.BlockSpec((GROUP_ROWS, d), lambda i, te, nu: (jnp.minimum(i, nu[0] - 1), 0)),
            pl.BlockSpec((sq, 1, d, two_ff), lambda i, te, nu: (layer, te[i], 0, 0)),
            pl.BlockSpec((sq, 1, 1, two_ff), lambda i, te, nu: (layer, te[i], 0, 0)),
            pl.BlockSpec((sq, 1, ff, d), lambda i, te, nu: (layer, te[i], 0, 0)),
            pl.BlockSpec((sq, 1, 1, d), lambda i, te, nu: (layer, te[i], 0, 0)),
        ],
        out_specs=pl.BlockSpec((GROUP_ROWS, d), lambda i, te, nu: (i, 0)),
        scratch_shapes=[pltpu.VMEM((d, two_ff), BF16), pltpu.VMEM((ff, d), BF16)],
    )
    return pl.pallas_call(
        _ffn_body,
        grid_spec=grid_spec,
        out_shape=jax.ShapeDtypeStruct((p, d), F32),
        compiler_params=pltpu.CompilerParams(dimension_semantics=("arbitrary",),
                                             vmem_limit_bytes=FFN_VMEM_LIMIT_BYTES),
        name="expert_ffn",
    )(tile_expert, n_used, xs, w_in, b_in.reshape(depth, e, 1, two_ff), w_out, b_out.reshape(depth, e, 1, d))


def _combine_body(*refs, final_norm):
    dest_hbm, route_ref, x_ref, mod_ref = refs[:4]
    pos = 4
    if final_norm:
        fg_ref = refs[pos]
        pos += 1
    y_hbm, o_ref, buf_ref, idx_ref, idx_sem, sem = refs[pos:]
    tokens, d = x_ref.shape[1], x_ref.shape[2]
    step = pl.program_id(0) * pl.num_programs(1) + pl.program_id(1)
    _load_slots(dest_hbm, idx_ref, idx_sem, step)

    _issue_rows(lambda r, k: _row_copy(y_hbm, idx_ref[r * TOP_K + k], buf_ref.at[k], r, sem), tokens)
    _wait_rows(buf_ref, sem)

    gate2 = mod_ref[0][:, 5 * d:6 * d]
    for c in range(0, tokens, COMBINE_ROWS):
        rows = slice(c, c + COMBINE_ROWS)
        route = route_ref[rows, :]
        f = jnp.zeros((COMBINE_ROWS, d), F32)
        for k in range(TOP_K):
            f = f + route[:, TOP_K + k:TOP_K + k + 1] * buf_ref[k, rows, :]
        out = x_ref[0, rows, :] + gate2 * f
        if final_norm:
            out = _rms(out, RMS_EPS) * fg_ref[...]
        o_ref[0, rows, :] = out


COMBINE_ROWS = 256


def _combine(dest_flat, route, x, mod, y, final_g=None):
    b, l, d = x.shape
    tokens = _step_tokens(l)
    assert tokens % COMBINE_ROWS == 0
    nl = l // tokens
    final_norm = final_g is not None
    in_specs = [
        pl.BlockSpec(memory_space=pl.ANY),
        pl.BlockSpec((tokens, LANES), lambda bi, i: (bi * nl + i, 0)),
        pl.BlockSpec((1, tokens, d), lambda bi, i: (bi, i, 0)),
        pl.BlockSpec((1, 1, mod.shape[-1]), lambda bi, i: (bi, 0, 0)),
    ]
    args = [dest_flat, route, x, mod]
    if final_norm:
        in_specs.append(pl.BlockSpec((1, d), lambda bi, i: (0, 0)))
        args.append(final_g.reshape(1, d))
    in_specs.append(pl.BlockSpec(memory_space=pl.ANY))
    args.append(y)
    return pl.pallas_call(
        functools.partial(_combine_body, final_norm=final_norm),
        grid=(b, nl),
        in_specs=in_specs,
        out_specs=pl.BlockSpec((1, tokens, d), lambda bi, i: (bi, i, 0)),
        out_shape=jax.ShapeDtypeStruct((b, l, d), F32),
        scratch_shapes=[pltpu.VMEM((TOP_K, tokens, d), F32),
                        pltpu.SMEM((tokens * TOP_K,), jnp.int32),
                        pltpu.SemaphoreType.DMA(()), pltpu.SemaphoreType.DMA(())],
        compiler_params=_params("arbitrary", "arbitrary"),
        name="combine",
    )(*args)


def _moe_plan(logits, n_experts):
    t = logits.shape[0]
    route, counts = _route(logits)
    ids = route[:, 0:TOP_K].astype(jnp.int32)
    rank = route[:, 2 * TOP_K:3 * TOP_K].astype(jnp.int32)
    counts = counts[0, :n_experts].astype(jnp.int32)
    padded = ((counts + GROUP_ROWS - 1) // GROUP_ROWS) * GROUP_ROWS
    before = jnp.arange(n_experts)[None, :] < jnp.arange(n_experts)[:, None]
    starts = jnp.sum(jnp.where(before, padded[None, :], 0), axis=1)
    ends = starts + padded
    dest = (starts[ids] + rank).reshape(-1)
    n_tiles = (t * TOP_K) // GROUP_ROWS + n_experts
    tile_start = jnp.arange(n_tiles, dtype=jnp.int32) * GROUP_ROWS
    tile_expert = jnp.sum((tile_start[:, None] >= ends[None, :]).astype(jnp.int32), axis=1)
    tile_expert = jnp.minimum(tile_expert, n_experts - 1)
    n_used = (ends[-1:] // GROUP_ROWS).astype(jnp.int32)
    return route, dest, counts, starts, tile_expert, n_used, n_tiles * GROUP_ROWS


def _moe_experts(tokens, logits, w_in, b_in, w_out, b_out, layer):
    route, dest, counts, starts, tile_expert, n_used, n_rows = _moe_plan(logits, w_in.shape[1])
    xs = _dispatch(counts, starts, dest, tokens, n_rows)
    ys = _expert_ffn(tile_expert, n_used, xs, w_in, b_in, w_out, b_out, layer)
    return route, dest, ys


def _pad_router(router_w, router_b):
    d, e = router_w.shape
    rw = jnp.zeros((d, LANES), BF16).at[:, :e].set(router_w.astype(BF16))
    rb = jnp.full((1, LANES), NEG_BIG, F32).at[0, :e].set(router_b)
    return rw, rb


def kernel(x, c, ctx, c_ctx, mod_w, mod_b, norm1_g, norm2_g, router_w, router_b, expert_w_in,
           expert_b_in, expert_w_out, expert_b_out, ab_w_in, ab_w_out, conv_w, conv_b, conv_ln_g,
           conv_ln_b, diff_lambda, diff_subln_g, gqa_w_qkv, gqa_w_out, gqa_q_norm_g, gqa_k_norm_g,
           final_norm_g):
    b, n, d = x.shape
    n_ctx = ctx.shape[1]
    n_experts = router_w.shape[-1]
    depth = mod_w.shape[0]
    assert depth == 2

    rows = -(-(b + 1) // SUBLANES) * SUBLANES
    c_all = jnp.zeros((rows, d), F32).at[:b].set(c).at[b].set(c_ctx)
    mod_all = _modulation(c_all, mod_w, mod_b)
    mod_lat = [mod_all[l, :b][:, None, :] for l in range(depth)]
    mod_ctx = [jnp.broadcast_to(mod_all[l, b][None, None, :], (b, 1, 6 * d)) for l in range(depth)]

    x_lat, x_ctx = x, ctx

    ch = d // 2
    diff_qk = (ab_w_in.shape[-1] - 2 * ch - ch) // 2
    segs0 = [
        dict(start=0, width=2 * ch, gain=False, rope=False, scale=1.0),
        dict(start=2 * ch, width=diff_qk, gain=False, rope=True, scale=DIFF_HEAD_DIM ** -0.5 * LOG2_E),
        dict(start=2 * ch + diff_qk, width=diff_qk, gain=False, rope=True, scale=1.0),
        dict(start=2 * ch + 2 * diff_qk, width=ch, gain=False, rope=False, scale=1.0),
    ]
    w_in0 = ab_w_in[0].astype(BF16)
    tables_d = _rope_tables(n, DIFF_HEAD_DIM)
    quarter_d = DIFF_HEAD_DIM // 4
    glu_c, q_c, k_c, v_c = _project(x_ctx, mod_ctx[0], norm1_g[0], w_in0, segs0, [], None, quarter_d,
                                    256, "project0_ctx")
    glu_l, q_l, k_l, v_l = _project(x_lat, mod_lat[0], norm1_g[0], w_in0, segs0, [], tables_d,
                                    quarter_d, 512, "project0_lat")
    conv_args = (conv_w[0], conv_b[0], conv_ln_g[0], conv_ln_b[0])
    a_c = _conv_branch(glu_c, *conv_args)
    a_l = _conv_branch(glu_l, *conv_args)
    lam_init = 0.8 - 0.6 * math.exp(-0.3 * 0)
    diff_extra = (diff_lambda[0], diff_subln_g[0].reshape(1, LANES))
    d_l = _attention(q_l, [k_c, k_l], [v_c, v_l], "diff", 256, diff_extra, lam_init)
    d_c = _attention(q_c, [k_c], [v_c], "diff", 128, diff_extra, lam_init)

    rw0, rb0 = _pad_router(router_w[0], router_b[0])
    w_out0 = ab_w_out[0].astype(BF16)
    x_ctx, h2_c, lg_c = _out_project([a_c, d_c], w_out0, x_ctx, mod_ctx[0], norm2_g[0], rw0, rb0, 256)
    x_lat, h2_l, lg_l = _out_project([a_l, d_l], w_out0, x_lat, mod_lat[0], norm2_g[0], rw0, rb0, 512)

    t_ctx = b * n_ctx
    tokens = jnp.concatenate([h2_c.reshape(t_ctx, d), h2_l.reshape(b * n, d)], axis=0)
    logits = jnp.concatenate([lg_c.reshape(t_ctx, LANES), lg_l.reshape(b * n, LANES)], axis=0)
    route, dest, ys = _moe_experts(tokens, logits, expert_w_in, expert_b_in, expert_w_out, expert_b_out, 0)
    x_ctx = _combine(dest[:t_ctx * TOP_K], route[:t_ctx], x_ctx, mod_ctx[0], ys)
    x_lat = _combine(dest[t_ctx * TOP_K:], route[t_ctx:], x_lat, mod_lat[0], ys)

    nq = gqa_w_out.shape[1]
    nk = (gqa_w_qkv.shape[-1] - nq) // 2
    segs1 = [
        dict(start=0, width=nq, gain=True, rope=True, scale=GQA_HEAD_DIM ** -0.5 * LOG2_E),
        dict(start=nq, width=nk, gain=True, rope=True, scale=1.0),
        dict(start=nq + nk, width=nk, gain=False, rope=False, scale=1.0),
    ]
    w_qkv = gqa_w_qkv[0].astype(BF16)
    gains1 = [gqa_q_norm_g[0], gqa_k_norm_g[0]]
    tables_g = _rope_tables(n, GQA_HEAD_DIM)
    quarter_g = GQA_HEAD_DIM // 4
    _, k_c, v_c = _project(x_ctx, mod_ctx[1], norm1_g[1], w_qkv, segs1, gains1, None, quarter_g,
                           256, "project1_ctx")
    q_l, k_l, v_l = _project(x_lat, mod_lat[1], norm1_g[1], w_qkv, segs1, gains1, tables_g,
                             quarter_g, 512, "project1_lat")
    o_l = _attention(q_l, [k_c, k_l], [v_c, v_l], "gqa", 128)
    rw1, rb1 = _pad_router(router_w[1], router_b[1])
    x_lat, h2_l, lg_l = _out_project([o_l], gqa_w_out[0].astype(BF16), x_lat, mod_lat[1], norm2_g[1],
                                     rw1, rb1, 512)
    tokens = h2_l.reshape(b * n, d)
    route, dest, ys = _moe_experts(tokens, lg_l.reshape(b * n, LANES), expert_w_in, expert_b_in,
                                   expert_w_out, expert_b_out, 1)
    return _combine(dest, route, x_lat, mod_lat[1], ys, final_g=final_norm_g)
```
